```python
import jax, jax.numpy as jnp
from jax import lax
import numpy as np


D_MODEL = 4096
BATCH = 1
SEQ = 16384
DEPTH = 4

GRID_W = 64
CTX_LEN = 256
EPS = 1e-6

NA_HEAD_DIM = 128
NA_WIDTH = D_MODEL // 2
NA_HEADS = NA_WIDTH // NA_HEAD_DIM
NA_WIN_R = 8
NA_WIN_C = 16
NA_SCALE = NA_HEAD_DIM ** -0.5
POOL_WINDOWS = (2, 4, 8, 16)
POOL_GROUPS = len(POOL_WINDOWS)
POOL_WIDTH = D_MODEL // 2
POOL_GROUP_DIM = POOL_WIDTH // POOL_GROUPS
AB_IN = 3 * NA_WIDTH + POOL_WIDTH
AB_OUT = NA_WIDTH + POOL_WIDTH
MLA_HEADS = D_MODEL // 128
MLA_Q_RANK = D_MODEL // 4
MLA_KV_RANK = 512
MLA_NOPE = 128
MLA_ROPE = 64
MLA_V = 128
MLA_DOWN = MLA_Q_RANK + MLA_KV_RANK + MLA_ROPE
MLA_SCALE = (MLA_NOPE + MLA_ROPE) ** -0.5
Q_BLOCK = 128
ROPE_THETA = 10000.0
ROPE_AXIS_FREQS = MLA_ROPE // 4
D_FF = 3 * D_MODEL // 2

kernel_name = 'hybrid_na_pool_mla_convglu_dit'


def rmsnorm(x, g):
    xf = x.astype(jnp.float32)
    y = xf * lax.rsqrt(jnp.mean(xf * xf, axis=-1, keepdims=True) + EPS)
    return (y * g.astype(jnp.float32)).astype(x.dtype)


def dense_attention(q, k, v, scale):
    s = jnp.einsum('bqhd,bkhd->bhqk', q, k, preferred_element_type=jnp.float32) * scale
    p = jax.nn.softmax(s, axis=-1).astype(v.dtype)
    return jnp.einsum('bhqk,bkhd->bqhd', p, v)


def neighbourhood_attention(q, k, v, k_ctx, v_ctx, rpb):
    B, N, H, Dh = q.shape
    rows = N // GRID_W
    kr = min(NA_WIN_R, rows)
    qg = q.reshape(B, rows, GRID_W, H, Dh).swapaxes(0, 1)
    kg = k.reshape(B, rows, GRID_W, H, Dh)
    vg = v.reshape(B, rows, GRID_W, H, Dh)
    col = jnp.arange(GRID_W)
    c0 = jnp.clip(col - NA_WIN_C // 2, 0, GRID_W - NA_WIN_C)
    col_ok = (col[None, :] >= c0[:, None]) & (col[None, :] < c0[:, None] + NA_WIN_C)
    dc = jnp.clip(col[None, :] - col[:, None] + NA_WIN_C - 1, 0, 2 * NA_WIN_C - 2)
    rpb = rpb.astype(jnp.float32)
    n_loc = kr * GRID_W

    def row_block(args):
        r, qb = args
        r0 = jnp.clip(r - NA_WIN_R // 2, 0, rows - kr)
        kb = lax.dynamic_slice_in_dim(kg, r0, kr, axis=1)
        vb = lax.dynamic_slice_in_dim(vg, r0, kr, axis=1)
        dr = r0 + jnp.arange(kr) - r + NA_WIN_R - 1
        bias = rpb[:, dr[:, None, None], dc[None, :, :]].transpose(0, 2, 1, 3)
        s_loc = jnp.einsum('bqhd,brkhd->bhqrk', qb, kb, preferred_element_type=jnp.float32) * NA_SCALE + bias
        s_loc = jnp.where(col_ok[:, None, :], s_loc, -jnp.inf).reshape(B, H, GRID_W, n_loc)
        s_ctx = jnp.einsum('bqhd,blhd->bhql', qb, k_ctx, preferred_element_type=jnp.float32) * NA_SCALE
        p = jax.nn.softmax(jnp.concatenate([s_loc, s_ctx], axis=-1), axis=-1).astype(v.dtype)
        p_loc = p[..., :n_loc].reshape(B, H, GRID_W, kr, GRID_W)
        return (jnp.einsum('bhqrk,brkhd->bqhd', p_loc, vb)
                + jnp.einsum('bhql,blhd->bqhd', p[..., n_loc:], v_ctx))

    o = lax.map(row_block, (jnp.arange(rows), qg))
    return o.swapaxes(0, 1).reshape(B, N, H, Dh)


def multiscale_pool(u, w_grp, ch_scale):
    B, n, C = u.shape
    uf = u.astype(jnp.float32)
    csum = jnp.concatenate([jnp.zeros((B, 1, C), jnp.float32), jnp.cumsum(uf, axis=1)], axis=1)
    t = jnp.arange(n)
    diffs = []
    for gi, w in enumerate(POOL_WINDOWS):
        lo = jnp.clip(t - w // 2, 0, n)
        hi = jnp.clip(t - w // 2 + w, 0, n)
        sl = slice(gi * POOL_GROUP_DIM, (gi + 1) * POOL_GROUP_DIM)
        cs = csum[..., sl]
        mean = (cs[:, hi] - cs[:, lo]) / (hi - lo).astype(jnp.float32)[None, :, None]
        diffs.append(mean - uf[..., sl])
    d = jnp.stack(diffs, axis=2).astype(u.dtype)
    y = jnp.einsum('bngc,gcd->bngd', d, w_grp).reshape(B, n, C)
    return y * ch_scale


def ab_mixer(hl, hc, w_in, rpb, pool_w, pool_scale, w_out, need_ctx):
    B, N, _ = hl.shape
    L = hc.shape[1]
    zl = hl @ w_in
    zc = hc @ w_in
    heads = lambda z, i: z[..., i * NA_WIDTH:(i + 1) * NA_WIDTH].reshape(z.shape[:2] + (NA_HEADS, NA_HEAD_DIM))
    k_c, v_c = heads(zc, 1), heads(zc, 2)
    o_a = neighbourhood_attention(heads(zl, 0), heads(zl, 1), heads(zl, 2), k_c, v_c, rpb)
    o_b = multiscale_pool(zl[..., 3 * NA_WIDTH:], pool_w, pool_scale)
    yl = jnp.concatenate([o_a.reshape(B, N, NA_WIDTH), o_b], axis=-1) @ w_out
    yc = None
    if need_ctx:
        o_ac = dense_attention(heads(zc, 0), k_c, v_c, NA_SCALE)
        o_bc = multiscale_pool(zc[..., 3 * NA_WIDTH:], pool_w, pool_scale)
        yc = jnp.concatenate([o_ac.reshape(B, L, NA_WIDTH), o_bc], axis=-1) @ w_out
    return yl, yc


def axial_rope_tables(n):
    t = jnp.arange(n)
    row = (t // GRID_W).astype(jnp.float32)
    col = (t % GRID_W).astype(jnp.float32)
    freqs = ROPE_THETA ** (-jnp.arange(ROPE_AXIS_FREQS, dtype=jnp.float32) / ROPE_AXIS_FREQS)
    ang = jnp.stack([row[:, None] * freqs, col[:, None] * freqs], axis=1)
    return jnp.cos(ang), jnp.sin(ang)


def apply_axial_rope(x, cos, sin):
    shp = x.shape
    xr = x.reshape(shp[:-1] + (2, 2, ROPE_AXIS_FREQS))
    x1, x2 = xr[..., 0, :], xr[..., 1, :]
    cos = cos.astype(x.dtype)
    sin = sin.astype(x.dtype)
    return jnp.stack([x1 * cos - x2 * sin, x1 * sin + x2 * cos], axis=-2).reshape(shp)


def mla_attend(q_nope, q_rope, ckv, k_rope, w_uk, w_uv):
    q_abs = jnp.einsum('bqhn,chn->bqhc', q_nope, w_uk)
    s = (jnp.einsum('bqhc,bkc->bhqk', q_abs, ckv, preferred_element_type=jnp.float32)
         + jnp.einsum('bqhr,bkr->bhqk', q_rope, k_rope, preferred_element_type=jnp.float32)) * MLA_SCALE
    p = jax.nn.softmax(s, axis=-1).astype(ckv.dtype)
    o_lat = jnp.einsum('bhqk,bkc->bqhc', p, ckv)
    return jnp.einsum('bqhc,chv->bqhv', o_lat, w_uv)


def mla_mixer(hl, hc, w_down, g_q, g_kv, w_uq, w_ukv, w_out, cos, sin, need_ctx):
    B, N, _ = hl.shape
    L = hc.shape[1]
    w_ukv = w_ukv.reshape(MLA_KV_RANK, MLA_HEADS, MLA_NOPE + MLA_V)
    w_uk, w_uv = w_ukv[..., :MLA_NOPE], w_ukv[..., MLA_NOPE:]
    zl = hl @ w_down
    zc = hc @ w_down

    def kv_part(z):
        return rmsnorm(z[..., MLA_Q_RANK:MLA_Q_RANK + MLA_KV_RANK], g_kv), z[..., MLA_Q_RANK + MLA_KV_RANK:]

    def q_part(z):
        q = (rmsnorm(z[..., :MLA_Q_RANK], g_q) @ w_uq).reshape(z.shape[:2] + (MLA_HEADS, MLA_NOPE + MLA_ROPE))
        return q[..., :MLA_NOPE], q[..., MLA_NOPE:]

    ckv_l, kr_l = kv_part(zl)
    kr_l = apply_axial_rope(kr_l, cos, sin)
    ckv_c, kr_c = kv_part(zc)
    qn_l, qr_l = q_part(zl)
    qr_l = apply_axial_rope(qr_l, cos[:, None], sin[:, None])
    ckv_all = jnp.concatenate([ckv_c, ckv_l], axis=1)
    kr_all = jnp.concatenate([kr_c, kr_l], axis=1)
    nblk = N // Q_BLOCK
    qn_b = qn_l.reshape(B, nblk, Q_BLOCK, MLA_HEADS, MLA_NOPE).swapaxes(0, 1)
    qr_b = qr_l.reshape(B, nblk, Q_BLOCK, MLA_HEADS, MLA_ROPE).swapaxes(0, 1)
    o = lax.map(lambda qs: mla_attend(qs[0], qs[1], ckv_all, kr_all, w_uk, w_uv), (qn_b, qr_b))
    yl = o.swapaxes(0, 1).reshape(B, N, MLA_HEADS * MLA_V) @ w_out
    yc = None
    if need_ctx:
        qn_c, qr_c = q_part(zc)
        yc = mla_attend(qn_c, qr_c, ckv_c, kr_c, w_uk, w_uv).reshape(B, L, MLA_HEADS * MLA_V) @ w_out
    return yl, yc


def conv_glu(h, w_up, conv_w, conv_b, w_down):
    u = h @ w_up
    gate, val = u[..., :D_FF], u[..., D_FF:]
    gp = jnp.pad(gate, ((0, 0), (1, 1), (0, 0)))
    gate = gp[:, :-2] * conv_w[0] + gp[:, 1:-1] * conv_w[1] + gp[:, 2:] * conv_w[2] + conv_b
    return (jax.nn.silu(gate) * val) @ w_down


def setup_inputs(seed: int = 0) -> dict:
    key = jax.random.key(seed)
    ks = jax.random.split(key, 32)
    n_even = (DEPTH + 1) // 2
    n_odd = DEPTH // 2
    nrm = lambda k, shape, s: jax.random.normal(k, shape, jnp.float32) * s
    gain = lambda k, shape: 1.0 + 0.05 * jax.random.normal(k, shape, jnp.float32)
    D = D_MODEL
    return {
        'x': nrm(ks[0], (BATCH, SEQ, D), 1.0),
        'c': nrm(ks[1], (BATCH, D), 1.0),
        'ctx': nrm(ks[2], (BATCH, CTX_LEN, D), 1.0),
        'c_ctx': nrm(ks[3], (D,), 1.0),
        'w_ada': nrm(ks[4], (DEPTH, D, 6 * D), 0.5 * D ** -0.5),
        'b_ada': nrm(ks[5], (DEPTH, 6 * D), 0.02),
        'g_mix_pre': gain(ks[6], (DEPTH, D)),
        'g_mix_post': gain(ks[7], (DEPTH, D)),
        'g_ffn_pre': gain(ks[8], (DEPTH, D)),
        'g_ffn_post': gain(ks[9], (DEPTH, D)),
        'ab_w_in': nrm(ks[10], (n_even, D, AB_IN), D ** -0.5),
        'na_rpb': nrm(ks[11], (n_even, NA_HEADS, 2 * NA_WIN_R - 1, 2 * NA_WIN_C - 1), 0.1),
        'pool_w': nrm(ks[12], (n_even, POOL_GROUPS, POOL_GROUP_DIM, POOL_GROUP_DIM), POOL_GROUP_DIM ** -0.5),
        'pool_scale': gain(ks[13], (n_even, POOL_WIDTH)),
        'ab_w_out': nrm(ks[14], (n_even, AB_OUT, D), AB_OUT ** -0.5),
        'mla_w_down': nrm(ks[15], (n_odd, D, MLA_DOWN), D ** -0.5),
        'mla_g_q': gain(ks[16], (n_odd, MLA_Q_RANK)),
        'mla_g_kv': gain(ks[17], (n_odd, MLA_KV_RANK)),
        'mla_w_uq': nrm(ks[18], (n_odd, MLA_Q_RANK, MLA_HEADS * (MLA_NOPE + MLA_ROPE)), MLA_Q_RANK ** -0.5),
        'mla_w_ukv': nrm(ks[19], (n_odd, MLA_KV_RANK, MLA_HEADS * (MLA_NOPE + MLA_V)), MLA_KV_RANK ** -0.5),
        'mla_w_out': nrm(ks[20], (n_odd, MLA_HEADS * MLA_V, D), (MLA_HEADS * MLA_V) ** -0.5),
        'ffn_w_up': nrm(ks[21], (DEPTH, D, 2 * D_FF), D ** -0.5),
        'ffn_conv_w': nrm(ks[22], (DEPTH, 3, D_FF), 3 ** -0.5),
        'ffn_conv_b': nrm(ks[23], (DEPTH, D_FF), 0.02),
        'ffn_w_down': nrm(ks[24], (DEPTH, D_FF, D), D_FF ** -0.5),
    }


def reference(x, c, ctx, c_ctx, w_ada, b_ada, g_mix_pre, g_mix_post, g_ffn_pre, g_ffn_post,
              ab_w_in, na_rpb, pool_w, pool_scale, ab_w_out,
              mla_w_down, mla_g_q, mla_g_kv, mla_w_uq, mla_w_ukv, mla_w_out,
              ffn_w_up, ffn_conv_w, ffn_conv_b, ffn_w_down):
    N = x.shape[1]
    cos, sin = axial_rope_tables(N)
    for l in range(DEPTH):
        last = l == DEPTH - 1
        mod_l = (jax.nn.silu(c) @ w_ada[l] + b_ada[l])[:, None, :]
        mod_c = (jax.nn.silu(c_ctx) @ w_ada[l] + b_ada[l])[None, None, :]
        sh1, sc1, gt1, sh2, sc2, gt2 = jnp.split(mod_l, 6, axis=-1)
        sh1c, sc1c, gt1c, sh2c, sc2c, gt2c = jnp.split(mod_c, 6, axis=-1)
        hl = rmsnorm(x, g_mix_pre[l]) * (1 + sc1) + sh1
        hc = rmsnorm(ctx, g_mix_pre[l]) * (1 + sc1c) + sh1c
        if l % 2 == 0:
            e = l // 2
            yl, yc = ab_mixer(hl, hc, ab_w_in[e], na_rpb[e], pool_w[e], pool_scale[e], ab_w_out[e], not last)
        else:
            o = l // 2
            yl, yc = mla_mixer(hl, hc, mla_w_down[o], mla_g_q[o], mla_g_kv[o], mla_w_uq[o], mla_w_ukv[o],
                               mla_w_out[o], cos, sin, not last)
        x = x + gt1 * rmsnorm(yl, g_mix_post[l])
        hl = rmsnorm(x, g_ffn_pre[l]) * (1 + sc2) + sh2
        x = x + gt2 * rmsnorm(conv_glu(hl, ffn_w_up[l], ffn_conv_w[l], ffn_conv_b[l], ffn_w_down[l]), g_ffn_post[l])
        if not last:
            ctx = ctx + gt1c * rmsnorm(yc, g_mix_post[l])
            hc = rmsnorm(ctx, g_ffn_pre[l]) * (1 + sc2c) + sh2c
            ctx = ctx + gt2c * rmsnorm(conv_glu(hc, ffn_w_up[l], ffn_conv_w[l], ffn_conv_b[l], ffn_w_down[l]), g_ffn_post[l])
    return x
```

```python
import functools

import jax
import jax.numpy as jnp
from jax import lax
from jax.experimental import pallas as pl
from jax.experimental.pallas import tpu as pltpu

GRID_W = 64
EPS = 1e-6
NA_HEAD_DIM = 128
NA_WIN_R = 8
NA_WIN_C = 16
NA_Q_ROWS = 4
NA_K_ROWS = NA_Q_ROWS + NA_WIN_R
POOL_WINDOWS = (2, 4, 8, 16)
POOL_HALO = 8
MLA_NOPE = 128
MLA_ROPE = 64
MLA_V = 128
ROPE_THETA = 10000.0
LANES = 128
NEG = -1e30
VMEM_LIMIT = 56 * 1024 * 1024

F32 = jnp.float32
BF16 = jnp.bfloat16


def _pick(n, prefs):
    for p in prefs:
        if n % p == 0:
            return p
    return n


def _params(*sem):
    return pltpu.CompilerParams(dimension_semantics=sem, vmem_limit_bytes=VMEM_LIMIT)


def _rms(x):
    return x * lax.rsqrt(jnp.mean(x * x, axis=-1, keepdims=True) + EPS)


def _dot(a, b):
    return jnp.dot(a, b, preferred_element_type=F32)


def _dot_nt(a, b):
    return lax.dot_general(a, b, (((1,), (1,)), ((), ())), preferred_element_type=F32)


def _ada_kernel(c_ref, w_ref, b_ref, o_ref):
    c = c_ref[...]
    s = (c * jax.nn.sigmoid(c)).astype(BF16)
    o_ref[...] = _dot(s, w_ref[...].astype(BF16)) + b_ref[...]


def _ada(cvec, w_ada, b_ada):
    depth, d, n6 = w_ada.shape
    rows = cvec.shape[0]
    tn = _pick(n6, (512, 256, 128))
    return pl.pallas_call(
        _ada_kernel,
        out_shape=jax.ShapeDtypeStruct((depth, rows, n6), F32),
        grid=(depth, n6 // tn),
        in_specs=[
            pl.BlockSpec((rows, d), lambda l, j: (0, 0)),
            pl.BlockSpec((None, d, tn), lambda l, j: (l, 0, j)),
            pl.BlockSpec((None, 1, tn), lambda l, j: (l, 0, j)),
        ],
        out_specs=pl.BlockSpec((None, rows, tn), lambda l, j: (l, 0, j)),
        compiler_params=_params("arbitrary", "arbitrary"),
        name="ada",
    )(cvec, w_ada, b_ada.reshape(depth, 1, n6))


def _norm_mod_kernel(x_ref, g_ref, sc_ref, sh_ref, o_ref):
    y = _rms(x_ref[...]) * g_ref[...]
    o_ref[...] = (y * (1.0 + sc_ref[...]) + sh_ref[...]).astype(o_ref.dtype)


def _norm_mod(x, g, sc, sh):
    m, d = x.shape
    tm = _pick(m, (256, 128, 64, 32, 16))
    vec = pl.BlockSpec((1, d), lambda i: (0, 0))
    return pl.pallas_call(
        _norm_mod_kernel,
        out_shape=jax.ShapeDtypeStruct((m, d), BF16),
        grid=(m // tm,),
        in_specs=[pl.BlockSpec((tm, d), lambda i: (i, 0)), vec, vec, vec],
        out_specs=pl.BlockSpec((tm, d), lambda i: (i, 0)),
        compiler_params=_params("arbitrary"),
        name="norm_mod",
    )(x, g, sc, sh)


def _resid_kernel(x_ref, y_ref, gpost_ref, gt_ref, *rest, with_h):
    xn = x_ref[...] + gt_ref[...] * (_rms(y_ref[...]) * gpost_ref[...])
    if with_h:
        gpre_ref, sc_ref, sh_ref, xo_ref, ho_ref = rest
        xo_ref[...] = xn
        h = _rms(xn) * gpre_ref[...]
        ho_ref[...] = (h * (1.0 + sc_ref[...]) + sh_ref[...]).astype(ho_ref.dtype)
    else:
        (xo_ref,) = rest
        xo_ref[...] = xn


def _resid(x, y, gpost, gt, pre=None):
    m, d = x.shape
    tm = _pick(m, (256, 128, 64, 32, 16))
    vec = pl.BlockSpec((1, d), lambda i: (0, 0))
    row = pl.BlockSpec((tm, d), lambda i: (i, 0))
    with_h = pre is not None
    out_shape = [jax.ShapeDtypeStruct((m, d), F32)]
    out_specs = [row]
    args = [x, y, gpost, gt]
    in_specs = [row, row, vec, vec]
    if with_h:
        out_shape.append(jax.ShapeDtypeStruct((m, d), BF16))
        out_specs.append(row)
        args += list(pre)
        in_specs += [vec, vec, vec]
    out = pl.pallas_call(
        functools.partial(_resid_kernel, with_h=with_h),
        out_shape=out_shape,
        grid=(m // tm,),
        in_specs=in_specs,
        out_specs=out_specs,
        compiler_params=_params("arbitrary"),
        name="resid",
    )(*args)
    return (out[0], out[1]) if with_h else (out[0], None)


def _mm_kernel(a_ref, w_ref, o_ref):
    o_ref[...] = _dot(a_ref[...], w_ref[...]).astype(o_ref.dtype)


def _mm(a, w, out_dtype):
    m, k = a.shape
    n = w.shape[1]
    tm = _pick(m, (1024, 1280, 512, 256, 128))
    tn = _pick(n, (1024, 512, 256, 128))
    return pl.pallas_call(
        _mm_kernel,
        out_shape=jax.ShapeDtypeStruct((m, n), out_dtype),
        grid=(m // tm, n // tn),
        in_specs=[pl.BlockSpec((tm, k), lambda i, j: (i, 0)),
                  pl.BlockSpec((k, tn), lambda i, j: (0, j))],
        out_specs=pl.BlockSpec((tm, tn), lambda i, j: (i, j)),
        compiler_params=_params("arbitrary", "arbitrary"),
        name="mm",
    )(a, w)


def _mm2_kernel(a1_ref, w1_ref, a2_ref, w2_ref, o_ref):
    o_ref[...] = (_dot(a1_ref[...], w1_ref[...]) + _dot(a2_ref[...], w2_ref[...])).astype(o_ref.dtype)


def _mm2(a1, w1, a2, w2, out_dtype):
    m, k1 = a1.shape
    k2 = a2.shape[1]
    n = w1.shape[1]
    tm = _pick(m, (1024, 512, 256, 128))
    tn = _pick(n, (1024, 512, 256, 128))
    return pl.pallas_call(
        _mm2_kernel,
        out_shape=jax.ShapeDtypeStruct((m, n), out_dtype),
        grid=(m // tm, n // tn),
        in_specs=[pl.BlockSpec((tm, k1), lambda i, j: (i, 0)),
                  pl.BlockSpec((k1, tn), lambda i, j: (0, j)),
                  pl.BlockSpec((tm, k2), lambda i, j: (i, 0)),
                  pl.BlockSpec((k2, tn), lambda i, j: (0, j))],
        out_specs=pl.BlockSpec((tm, tn), lambda i, j: (i, j)),
        compiler_params=_params("arbitrary", "arbitrary"),
        name="mm2",
    )(a1, w1, a2, w2)


def _na_bias(rpb, rows):
    w = GRID_W
    h = rpb.shape[0]
    qr = jnp.arange(NA_Q_ROWS)[:, None, None, None]
    qc = jnp.arange(w)[None, :, None, None]
    kr = jnp.arange(NA_K_ROWS)[None, None, :, None]
    kc = jnp.arange(w)[None, None, None, :]
    c0 = jnp.clip(qc - NA_WIN_C // 2, 0, w - NA_WIN_C)
    ok_c = (kc >= c0) & (kc < c0 + NA_WIN_C)
    dc = jnp.clip(kc - qc + NA_WIN_C - 1, 0, 2 * NA_WIN_C - 2)
    out = []
    for r_first, k_first in ((0, 0), (NA_Q_ROWS, 0), (rows - NA_Q_ROWS, rows - NA_K_ROWS)):
        r = r_first + qr
        ka = k_first + kr
        r0 = jnp.clip(r - NA_WIN_R // 2, 0, rows - NA_WIN_R)
        ok_r = (ka >= r0) & (ka < r0 + NA_WIN_R)
        dr = jnp.clip(ka - r + NA_WIN_R - 1, 0, 2 * NA_WIN_R - 2)
        b = rpb.astype(F32)[:, dr, dc]
        b = jnp.where(ok_r & ok_c, b, NEG)
        out.append(b.reshape(h, NA_Q_ROWS * w, NA_K_ROWS * w))
    return jnp.stack(out)


def _na_kernel(q_ref, k0_ref, k1_ref, k2_ref, v0_ref, v1_ref, v2_ref, kc_ref, vc_ref, b_ref, o_ref,
               *, scale):
    q = q_ref[...]
    s_loc = jnp.concatenate([_dot_nt(q, k_ref[...]) for k_ref in (k0_ref, k1_ref, k2_ref)], axis=1)
    s_loc = s_loc * scale + b_ref[...]
    s_ctx = _dot_nt(q, kc_ref[...]) * scale
    m = jnp.maximum(jnp.max(s_loc, axis=1, keepdims=True), jnp.max(s_ctx, axis=1, keepdims=True))
    p_loc = jnp.exp(s_loc - m)
    p_ctx = jnp.exp(s_ctx - m)
    denom = jnp.sum(p_loc, axis=1, keepdims=True) + jnp.sum(p_ctx, axis=1, keepdims=True)
    qb = q.shape[0]
    o = _dot(p_ctx.astype(BF16), vc_ref[...])
    for j, v_ref in enumerate((v0_ref, v1_ref, v2_ref)):
        o = o + _dot(p_loc[:, j * qb:(j + 1) * qb].astype(BF16), v_ref[...])
    o_ref[...] = (o / denom).astype(o_ref.dtype)


def _na_attention(qkv_l, qkv_c, bias):
    n = qkv_l.shape[0]
    lc = qkv_c.shape[0]
    width = qkv_l.shape[1] // 3
    heads = width // NA_HEAD_DIM
    qb = NA_Q_ROWS * GRID_W
    nblk = n // qb
    assert n % qb == 0 and nblk >= 3

    def kv_spec(j, col0):
        return pl.BlockSpec(
            (qb, NA_HEAD_DIM),
            lambda h, i: (jnp.clip(i - 1, 0, nblk - 3) + j, col0 + h))

    def bias_map(h, i):
        return (jnp.where(i == 0, 0, jnp.where(i == nblk - 1, 2, 1)), h, 0, 0)

    return pl.pallas_call(
        functools.partial(_na_kernel, scale=NA_HEAD_DIM ** -0.5),
        out_shape=jax.ShapeDtypeStruct((n, width), BF16),
        grid=(heads, nblk),
        in_specs=[
            pl.BlockSpec((qb, NA_HEAD_DIM), lambda h, i: (i, h)),
            kv_spec(0, heads), kv_spec(1, heads), kv_spec(2, heads),
            kv_spec(0, 2 * heads), kv_spec(1, 2 * heads), kv_spec(2, 2 * heads),
            pl.BlockSpec((lc, NA_HEAD_DIM), lambda h, i: (0, heads + h)),
            pl.BlockSpec((lc, NA_HEAD_DIM), lambda h, i: (0, 2 * heads + h)),
            pl.BlockSpec((None, None, qb, 3 * qb), bias_map),
        ],
        out_specs=pl.BlockSpec((qb, NA_HEAD_DIM), lambda h, i: (i, h)),
        compiler_params=_params("arbitrary", "arbitrary"),
        name="na_attention",
    )(qkv_l, qkv_l, qkv_l, qkv_l, qkv_l, qkv_l, qkv_l, qkv_c, qkv_c, bias)


def _dense_attn_kernel(q_ref, k_ref, v_ref, o_ref, *, scale):
    s = _dot_nt(q_ref[...], k_ref[...]) * scale
    p = jnp.exp(s - jnp.max(s, axis=1, keepdims=True))
    o = _dot(p.astype(BF16), v_ref[...])
    o_ref[...] = (o / jnp.sum(p, axis=1, keepdims=True)).astype(o_ref.dtype)


def _dense_attention(qkv):
    n = qkv.shape[0]
    width = qkv.shape[1] // 3
    heads = width // NA_HEAD_DIM
    return pl.pallas_call(
        functools.partial(_dense_attn_kernel, scale=NA_HEAD_DIM ** -0.5),
        out_shape=jax.ShapeDtypeStruct((n, width), BF16),
        grid=(heads,),
        in_specs=[pl.BlockSpec((n, NA_HEAD_DIM), lambda h: (0, h)),
                  pl.BlockSpec((n, NA_HEAD_DIM), lambda h: (0, heads + h)),
                  pl.BlockSpec((n, NA_HEAD_DIM), lambda h: (0, 2 * heads + h))],
        out_specs=pl.BlockSpec((n, NA_HEAD_DIM), lambda h: (0, h)),
        compiler_params=_params("arbitrary"),
        name="dense_attention",
    )(qkv, qkv, qkv)


def _pool_kernel(u_ref, o_ref, *, n, chunk, blocks_per_group):
    group = pl.program_id(0) // blocks_per_group
    rows = chunk + 2 * POOL_HALO

    def window_sum(xs, w):
        a = pltpu.roll(xs, 1, axis=0) + xs
        half = 1
        while 2 * half < w:
            a = pltpu.roll(a, half, axis=0) + pltpu.roll(a, rows - half, axis=0)
            half *= 2
        return a

    for gi, w in enumerate(POOL_WINDOWS):
        @pl.when(group == gi)
        def _(w=w):
            def body(c, carry):
                base = pl.multiple_of(c * chunk, chunk)
                xs = u_ref[pl.ds(base, rows), :]
                tot = window_sum(xs, w)[POOL_HALO:POOL_HALO + chunk]
                t = base + lax.broadcasted_iota(jnp.int32, (chunk, 1), 0)
                cnt = jnp.minimum(t + w // 2, n) - jnp.maximum(t - w // 2, 0)
                mean = tot / cnt.astype(F32)
                o_ref[pl.ds(base, chunk), :] = (mean - xs[POOL_HALO:POOL_HALO + chunk]).astype(o_ref.dtype)
                return carry
            lax.fori_loop(0, n // chunk, body, 0)


def _pool_diff(u):
    n, c = u.shape
    cg = c // len(POOL_WINDOWS)
    assert cg % LANES == 0
    chunk = _pick(n, (512, 256, 128))
    up = jnp.pad(u, ((POOL_HALO, POOL_HALO), (0, 0)))
    return pl.pallas_call(
        functools.partial(_pool_kernel, n=n, chunk=chunk, blocks_per_group=cg // LANES),
        out_shape=jax.ShapeDtypeStruct((n, c), BF16),
        grid=(c // LANES,),
        in_specs=[pl.BlockSpec((n + 2 * POOL_HALO, LANES), lambda j: (0, j))],
        out_specs=pl.BlockSpec((n, LANES), lambda j: (0, j)),
        compiler_params=_params("arbitrary"),
        name="pool_diff",
    )(up)


def _group_mm_kernel(a_ref, w_ref, s_ref, o_ref):
    o_ref[...] = (_dot(a_ref[...], w_ref[...]) * s_ref[...]).astype(o_ref.dtype)


def _group_mm(d, w_grp, ch_scale):
    n, c = d.shape
    g, cg, _ = w_grp.shape
    tm = _pick(n, (1024, 512, 256, 128))
    return pl.pallas_call(
        _group_mm_kernel,
        out_shape=jax.ShapeDtypeStruct((n, c), BF16),
        grid=(g, n // tm),
        in_specs=[pl.BlockSpec((tm, cg), lambda gi, i: (i, gi)),
                  pl.BlockSpec((None, cg, cg), lambda gi, i: (gi, 0, 0)),
                  pl.BlockSpec((1, cg), lambda gi, i: (0, gi))],
        out_specs=pl.BlockSpec((tm, cg), lambda gi, i: (i, gi)),
        compiler_params=_params("arbitrary", "arbitrary"),
        name="group_mm",
    )(d, w_grp, ch_scale)


def _rope_tables(n):
    nf = MLA_ROPE // 4
    t = jnp.arange(n)
    row = (t // GRID_W).astype(F32)
    col = (t % GRID_W).astype(F32)
    freqs = ROPE_THETA ** (-jnp.arange(nf, dtype=F32) / nf)
    ang = jnp.stack([row[:, None] * freqs, col[:, None] * freqs], axis=1)
    cos, sin = jnp.cos(ang), jnp.sin(ang)
    c = jnp.stack([cos, cos], axis=2).reshape(n, MLA_ROPE)
    s = jnp.stack([-sin, sin], axis=2).reshape(n, MLA_ROPE)
    pad = jnp.zeros((n, LANES - MLA_ROPE), F32)
    return jnp.concatenate([c, pad], axis=1), jnp.concatenate([s, pad], axis=1)


def _rope_cols(w):
    nf = MLA_ROPE // 4
    j = jnp.arange(MLA_ROPE)
    partner = (j // (2 * nf)) * (2 * nf) + (1 - (j // nf) % 2) * nf + j % nf
    z = jnp.zeros((w.shape[0], LANES - MLA_ROPE), w.dtype)
    return jnp.concatenate([w, z, w[:, partner], z], axis=1)


def _mla_down_kernel(h_ref, w_ref, gq_ref, gkv_ref, c_ref, s_ref, qn_ref, ckv_ref, kr_ref, *, qr, kvr):
    z = _dot(h_ref[...], w_ref[...])
    qn_ref[...] = (_rms(z[:, :qr]) * gq_ref[...]).astype(qn_ref.dtype)
    ckv_ref[...] = (_rms(z[:, qr:qr + kvr]) * gkv_ref[...]).astype(ckv_ref.dtype)
    r0 = qr + kvr
    rot = z[:, r0:r0 + LANES] * c_ref[...] + z[:, r0 + LANES:r0 + 2 * LANES] * s_ref[...]
    kr_ref[...] = rot.astype(kr_ref.dtype)


def _mla_down(h, w_ext, g_q, g_kv, ctab, stab):
    n, d = h.shape
    qr = g_q.shape[1]
    kvr = g_kv.shape[1]
    nw = w_ext.shape[1]
    tm = _pick(n, (512, 256, 128))
    return pl.pallas_call(
        functools.partial(_mla_down_kernel, qr=qr, kvr=kvr),
        out_shape=[jax.ShapeDtypeStruct((n, qr), BF16),
                   jax.ShapeDtypeStruct((n, kvr), BF16),
                   jax.ShapeDtypeStruct((n, LANES), BF16)],
        grid=(n // tm,),
        in_specs=[pl.BlockSpec((tm, d), lambda i: (i, 0)),
                  pl.BlockSpec((d, nw), lambda i: (0, 0)),
                  pl.BlockSpec((1, qr), lambda i: (0, 0)),
                  pl.BlockSpec((1, kvr), lambda i: (0, 0)),
                  pl.BlockSpec((tm, LANES), lambda i: (i, 0)),
                  pl.BlockSpec((tm, LANES), lambda i: (i, 0))],
        out_specs=[pl.BlockSpec((tm, qr), lambda i: (i, 0)),
                   pl.BlockSpec((tm, kvr), lambda i: (i, 0)),
                   pl.BlockSpec((tm, LANES), lambda i: (i, 0))],
        compiler_params=_params("arbitrary"),
        name="mla_down",
    )(h, w_ext, g_q, g_kv, ctab, stab)


MLA_Q_IN = MLA_NOPE + 2 * LANES
MLA_Q_OUT = MLA_NOPE + LANES


def _q_up_kernel(a_ref, w_ref, c_ref, s_ref, o_ref, *, heads, scale):
    z = _dot(a_ref[...], w_ref[...])
    c = c_ref[...] * scale
    s = s_ref[...] * scale
    for j in range(heads):
        zi = j * MLA_Q_IN
        oi = j * MLA_Q_OUT
        o_ref[:, oi:oi + MLA_NOPE] = (z[:, zi:zi + MLA_NOPE] * scale).astype(o_ref.dtype)
        rot = z[:, zi + MLA_NOPE:zi + MLA_NOPE + LANES] * c + z[:, zi + MLA_NOPE + LANES:zi + MLA_Q_IN] * s
        o_ref[:, oi + MLA_NOPE:oi + MLA_Q_OUT] = rot.astype(o_ref.dtype)


def _q_up(qn, w_ext, ctab, stab, heads):
    n, qr = qn.shape
    tm = _pick(n, (1024, 512, 256, 128))
    hb = _pick(heads, (4, 2, 1))
    scale = (MLA_NOPE + MLA_ROPE) ** -0.5
    return pl.pallas_call(
        functools.partial(_q_up_kernel, heads=hb, scale=scale),
        out_shape=jax.ShapeDtypeStruct((n, heads * MLA_Q_OUT), BF16),
        grid=(n // tm, heads // hb),
        in_specs=[pl.BlockSpec((tm, qr), lambda i, j: (i, 0)),
                  pl.BlockSpec((qr, hb * MLA_Q_IN), lambda i, j: (0, j)),
                  pl.BlockSpec((tm, LANES), lambda i, j: (i, 0)),
                  pl.BlockSpec((tm, LANES), lambda i, j: (i, 0))],
        out_specs=pl.BlockSpec((tm, hb * MLA_Q_OUT), lambda i, j: (i, j)),
        compiler_params=_params("arbitrary", "arbitrary"),
        name="mla_q_up",
    )(qn, w_ext, ctab, stab)


def _mla_attn_kernel(q_ref, kv_ref, kr_ref, *rest, tk, has_ctx):
    if has_ctx:
        kvc_ref, krc_ref, o_ref, m_sc, l_sc, acc_sc = rest
    else:
        o_ref, m_sc, l_sc, acc_sc = rest
    q = q_ref[...]
    m_sc[...] = jnp.full(m_sc.shape, NEG, F32)
    l_sc[...] = jnp.zeros(l_sc.shape, F32)
    acc_sc[...] = jnp.zeros(acc_sc.shape, F32)

    def step(k_nope, k_rope, v):
        k = jnp.concatenate([k_nope, k_rope], axis=1)
        s = _dot_nt(q, k)
        m_prev = m_sc[...]
        m_new = jnp.maximum(m_prev, jnp.max(s, axis=1, keepdims=True))
        alpha = jnp.exp(m_prev - m_new)
        p = jnp.exp(s - m_new)
        l_sc[...] = alpha * l_sc[...] + jnp.sum(p, axis=1, keepdims=True)
        acc_sc[...] = alpha * acc_sc[...] + _dot(p.astype(BF16), v)
        m_sc[...] = m_new

    def body(c, carry):
        ks = pl.multiple_of(c * tk, tk)
        step(kv_ref[pl.ds(ks, tk), 0:MLA_NOPE], kr_ref[pl.ds(ks, tk), :],
             kv_ref[pl.ds(ks, tk), MLA_NOPE:MLA_NOPE + MLA_V])
        return carry

    lax.fori_loop(0, kv_ref.shape[0] // tk, body, 0)
    if has_ctx:
        step(kvc_ref[:, 0:MLA_NOPE], krc_ref[...], kvc_ref[:, MLA_NOPE:MLA_NOPE + MLA_V])
    o_ref[...] = (acc_sc[...] / l_sc[...]).astype(o_ref.dtype)


def _mla_attention(q, kv, kr, kv_c=None, kr_c=None):
    nq = q.shape[0]
    s = kv.shape[0]
    heads = q.shape[1] // MLA_Q_OUT
    tq = _pick(nq, (512, 256, 128))
    tk = _pick(s, (512, 256, 128))
    has_ctx = kv_c is not None
    kvw = MLA_NOPE + MLA_V
    in_specs = [pl.BlockSpec((tq, MLA_Q_OUT), lambda h, i: (i, h)),
                pl.BlockSpec((s, kvw), lambda h, i: (0, h)),
                pl.BlockSpec((s, LANES), lambda h, i: (0, 0))]
    args = [q, kv, kr]
    if has_ctx:
        sc = kv_c.shape[0]
        in_specs += [pl.BlockSpec((sc, kvw), lambda h, i: (0, h)),
                     pl.BlockSpec((sc, LANES), lambda h, i: (0, 0))]
        args += [kv_c, kr_c]
    return pl.pallas_call(
        functools.partial(_mla_attn_kernel, tk=tk, has_ctx=has_ctx),
        out_shape=jax.ShapeDtypeStruct((nq, heads * MLA_V), BF16),
        grid=(heads, nq // tq),
        in_specs=in_specs,
        out_specs=pl.BlockSpec((tq, MLA_V), lambda h, i: (i, h)),
        scratch_shapes=[pltpu.VMEM((tq, 1), F32), pltpu.VMEM((tq, 1), F32), pltpu.VMEM((tq, MLA_V), F32)],
        compiler_params=_params("arbitrary", "arbitrary"),
        name="mla_attention",
    )(*args)


GLU_HALO = 16


def _glu_down_kernel(g_ref, v_ref, gp_ref, gn_ref, cw_ref, cb_ref, w_ref, o_ref):
    i = pl.program_id(0)
    k = pl.program_id(1)
    tm = g_ref.shape[0]
    g = g_ref[...].astype(F32)
    prev_row = jnp.where(i > 0, gp_ref[GLU_HALO - 1:GLU_HALO, :].astype(F32), 0.0)
    next_row = jnp.where(i < pl.num_programs(0) - 1, gn_ref[0:1, :].astype(F32), 0.0)
    row = lax.broadcasted_iota(jnp.int32, (tm, 1), 0)
    g_dn = jnp.where(row == 0, prev_row, pltpu.roll(g, 1, axis=0))
    g_up = jnp.where(row == tm - 1, next_row, pltpu.roll(g, tm - 1, axis=0))
    cw = cw_ref[...]
    gc = g_dn * cw[0:1, :] + g * cw[1:2, :] + g_up * cw[2:3, :] + cb_ref[...]
    a = (gc * jax.nn.sigmoid(gc)) * v_ref[...].astype(F32)
    part = _dot(a.astype(BF16), w_ref[...])

    @pl.when(k == 0)
    def _():
        o_ref[...] = part

    @pl.when(k > 0)
    def _():
        o_ref[...] += part


def _glu_down(u, conv_w, conv_b, w_down):
    n = u.shape[0]
    dff, d = w_down.shape
    tm = _pick(n, (512, 256, 128))
    tk = _pick(dff, (1024, 512, 256, 128))
    nk = dff // tk
    hb = tm // GLU_HALO
    nhb = n // GLU_HALO
    return pl.pallas_call(
        _glu_down_kernel,
        out_shape=jax.ShapeDtypeStruct((n, d), F32),
        grid=(n // tm, nk),
        in_specs=[pl.BlockSpec((tm, tk), lambda i, k: (i, k)),
                  pl.BlockSpec((tm, tk), lambda i, k: (i, nk + k)),
                  pl.BlockSpec((GLU_HALO, tk), lambda i, k: (jnp.maximum(i * hb - 1, 0), k)),
                  pl.BlockSpec((GLU_HALO, tk), lambda i, k: (jnp.minimum((i + 1) * hb, nhb - 1), k)),
                  pl.BlockSpec((3, tk), lambda i, k: (0, k)),
                  pl.BlockSpec((1, tk), lambda i, k: (0, k)),
                  pl.BlockSpec((tk, d), lambda i, k: (k, 0))],
        out_specs=pl.BlockSpec((tm, d), lambda i, k: (i, 0)),
        compiler_params=_params("arbitrary", "arbitrary"),
        name="glu_down",
    )(u, u, u, u, conv_w, conv_b, w_down)


def _conv_glu(h, w_up, conv_w, conv_b, w_down):
    return _glu_down(_mm(h, w_up, BF16), conv_w, conv_b, w_down)


def _ab_mixer(hl, hc, w_in, rpb, pool_w, pool_scale, w_out, need_ctx):
    n = hl.shape[0]
    naw = w_out.shape[0] // 2
    w_qkv = w_in[:, :3 * naw].astype(BF16)
    w_u = w_in[:, 3 * naw:].astype(BF16)
    w_out_a = w_out[:naw].astype(BF16)
    w_out_b = w_out[naw:].astype(BF16)
    pool_w = pool_w.astype(BF16)
    pool_scale = pool_scale.reshape(1, -1)
    qkv_l = _mm(hl, w_qkv, BF16)
    qkv_c = _mm(hc, w_qkv, BF16)
    o_a = _na_attention(qkv_l, qkv_c, _na_bias(rpb, n // GRID_W))
    o_b = _group_mm(_pool_diff(_mm(hl, w_u, F32)), pool_w, pool_scale)
    yl = _mm2(o_a, w_out_a, o_b, w_out_b, F32)
    yc = None
    if need_ctx:
        o_ac = _dense_attention(qkv_c)
        o_bc = _group_mm(_pool_diff(_mm(hc, w_u, F32)), pool_w, pool_scale)
        yc = _mm2(o_ac, w_out_a, o_bc, w_out_b, F32)
    return yl, yc


def _mla_mixer(hl, hc, w_down, g_q, g_kv, w_uq, w_ukv, w_out, tabs_l, tabs_c, need_ctx):
    qr = g_q.shape[0]
    kvr = g_kv.shape[0]
    heads = w_ukv.shape[1] // (MLA_NOPE + MLA_V)
    w_down_ext = jnp.concatenate([w_down[:, :qr + kvr], _rope_cols(w_down[:, qr + kvr:])], axis=1).astype(BF16)
    w_uq_h = w_uq.reshape(qr, heads, MLA_NOPE + MLA_ROPE)
    w_uq_ext = jnp.concatenate(
        [w_uq_h[..., :MLA_NOPE],
         _rope_cols(w_uq_h[..., MLA_NOPE:].reshape(qr * heads, MLA_ROPE)).reshape(qr, heads, 2 * LANES)],
        axis=-1).reshape(qr, heads * MLA_Q_IN).astype(BF16)
    w_ukv = w_ukv.astype(BF16)
    w_out = w_out.astype(BF16)
    g_q = g_q.reshape(1, -1)
    g_kv = g_kv.reshape(1, -1)
    qn_l, ckv_l, kr_l = _mla_down(hl, w_down_ext, g_q, g_kv, *tabs_l)
    qn_c, ckv_c, kr_c = _mla_down(hc, w_down_ext, g_q, g_kv, *tabs_c)
    q_l = _q_up(qn_l, w_uq_ext, *tabs_l, heads)
    kv_l = _mm(ckv_l, w_ukv, BF16)
    kv_c = _mm(ckv_c, w_ukv, BF16)
    yl = _mm(_mla_attention(q_l, kv_l, kr_l, kv_c, kr_c), w_out, F32)
    yc = None
    if need_ctx:
        q_c = _q_up(qn_c, w_uq_ext, *tabs_c, heads)
        yc = _mm(_mla_attention(q_c, kv_c, kr_c), w_out, F32)
    return yl, yc


def kernel(x, c, ctx, c_ctx, w_ada, b_ada, g_mix_pre, g_mix_post, g_ffn_pre, g_ffn_post, ab_w_in, na_rpb, pool_w, pool_scale, ab_w_out, mla_w_down, mla_g_q, mla_g_kv, mla_w_uq, mla_w_ukv, mla_w_out, ffn_w_up, ffn_conv_w, ffn_conv_b, ffn_w_down):
    batch, n, d = x.shape
    lc = ctx.shape[1]
    depth = w_ada.shape[0]
    assert batch == 1 and c.shape[0] == 1
    xl = x.reshape(n, d)
    xc = ctx.reshape(lc, d)

    cvec = jnp.concatenate([c, c_ctx[None, :], jnp.zeros((6, d), F32)], axis=0)
    mod = _ada(cvec, w_ada, b_ada)

    def mods(l, r):
        return [mod[l, r:r + 1, j * d:(j + 1) * d] for j in range(6)]

    tabs_l = _rope_tables(n)
    tabs_c = (jnp.concatenate([jnp.ones((lc, MLA_ROPE), F32), jnp.zeros((lc, LANES - MLA_ROPE), F32)], axis=1),
              jnp.zeros((lc, LANES), F32))

    vec = lambda a: a.reshape(1, -1)
    sh1, sc1, _, _, _, _ = mods(0, 0)
    sh1c, sc1c, _, _, _, _ = mods(0, 1)
    hl = _norm_mod(xl, vec(g_mix_pre[0]), sc1, sh1)
    hc = _norm_mod(xc, vec(g_mix_pre[0]), sc1c, sh1c)
    for l in range(depth):
        last = l == depth - 1
        sh1, sc1, gt1, sh2, sc2, gt2 = mods(l, 0)
        sh1c, sc1c, gt1c, sh2c, sc2c, gt2c = mods(l, 1)
        if l % 2 == 0:
            e = l // 2
            yl, yc = _ab_mixer(hl, hc, ab_w_in[e], na_rpb[e], pool_w[e], pool_scale[e], ab_w_out[e], not last)
        else:
            o = l // 2
            yl, yc = _mla_mixer(hl, hc, mla_w_down[o], mla_g_q[o], mla_g_kv[o], mla_w_uq[o], mla_w_ukv[o],
                                mla_w_out[o], tabs_l, tabs_c, not last)
        w_up = ffn_w_up[l].astype(BF16)
        w_dn = ffn_w_down[l].astype(BF16)
        conv_b = vec(ffn_conv_b[l])
        xl, h2 = _resid(xl, yl, vec(g_mix_post[l]), gt1, (vec(g_ffn_pre[l]), sc2, sh2))
        f = _conv_glu(h2, w_up, ffn_conv_w[l], conv_b, w_dn)
        if last:
            xl, _ = _resid(xl, f, vec(g_ffn_post[l]), gt2)
        else:
            nsh1, nsc1, _, _, _, _ = mods(l + 1, 0)
            xl, hl = _resid(xl, f, vec(g_ffn_post[l]), gt2, (vec(g_mix_pre[l + 1]), nsc1, nsh1))
            xc, h2c = _resid(xc, yc, vec(g_mix_post[l]), gt1c, (vec(g_ffn_pre[l]), sc2c, sh2c))
            fc = _conv_glu(h2c, w_up, ffn_conv_w[l], conv_b, w_dn)
            nsh1c, nsc1c, _, _, _, _ = mods(l + 1, 1)
            xc, hc = _resid(xc, fc, vec(g_ffn_post[l]), gt2c, (vec(g_mix_pre[l + 1]), nsc1c, nsh1c))
    return xl.reshape(batch, n, d)
```

```python
import functools

import numpy as np
import jax
import jax.numpy as jnp
from jax import lax
from jax.experimental import pallas as pl
from jax.experimental.pallas import tpu as pltpu

GRID_W = 64
EPS = 1e-6
NA_HEAD_DIM = 128
NA_WIN_R = 8
NA_WIN_C = 16
NA_Q_ROWS = 4
NA_K_ROWS = NA_Q_ROWS + NA_WIN_R
POOL_WINDOWS = (2, 4, 8, 16)
POOL_HALO = 8
MLA_NOPE = 128
MLA_ROPE = 64
MLA_V = 128
ROPE_THETA = 10000.0
LANES = 128
NEG = -1e30
LOG2E = 1.4426950408889634
VMEM_LIMIT = 56 * 1024 * 1024

F32 = jnp.float32
BF16 = jnp.bfloat16


def _pick(n, prefs):
    for p in prefs:
        if n % p == 0:
            return p
    return n


def _params(*sem):
    return pltpu.CompilerParams(dimension_semantics=sem, vmem_limit_bytes=VMEM_LIMIT)


def _rms(x):
    return x * lax.rsqrt(jnp.mean(x * x, axis=-1, keepdims=True) + EPS)


def _dot(a, b):
    return jnp.dot(a, b, preferred_element_type=F32)


def _dot_nt(a, b):
    return lax.dot_general(a, b, (((1,), (1,)), ((), ())), preferred_element_type=F32)


def _ada_kernel(c_ref, w_ref, b_ref, o_ref):
    c = c_ref[...]
    s = (c * jax.nn.sigmoid(c)).astype(BF16)
    o_ref[...] = _dot(s, w_ref[...].astype(BF16)) + b_ref[...]


def _ada(cvec, w_ada, b_ada):
    depth, d, n6 = w_ada.shape
    rows = cvec.shape[0]
    tn = _pick(n6, (512, 256, 128))
    return pl.pallas_call(
        _ada_kernel,
        out_shape=jax.ShapeDtypeStruct((depth, rows, n6), F32),
        grid=(depth, n6 // tn),
        in_specs=[
            pl.BlockSpec((rows, d), lambda l, j: (0, 0)),
            pl.BlockSpec((None, d, tn), lambda l, j: (l, 0, j)),
            pl.BlockSpec((None, 1, tn), lambda l, j: (l, 0, j)),
        ],
        out_specs=pl.BlockSpec((None, rows, tn), lambda l, j: (l, 0, j)),
        compiler_params=_params("arbitrary", "arbitrary"),
        name="ada",
    )(cvec, w_ada, b_ada.reshape(depth, 1, n6))


def _norm_mod_kernel(x_ref, g_ref, sc_ref, sh_ref, o_ref):
    y = _rms(x_ref[...]) * g_ref[...]
    o_ref[...] = (y * (1.0 + sc_ref[...]) + sh_ref[...]).astype(o_ref.dtype)


def _norm_mod(x, g, sc, sh):
    m, d = x.shape
    tm = _pick(m, (256, 128, 64, 32, 16))
    vec = pl.BlockSpec((1, d), lambda i: (0, 0))
    return pl.pallas_call(
        _norm_mod_kernel,
        out_shape=jax.ShapeDtypeStruct((m, d), BF16),
        grid=(m // tm,),
        in_specs=[pl.BlockSpec((tm, d), lambda i: (i, 0)), vec, vec, vec],
        out_specs=pl.BlockSpec((tm, d), lambda i: (i, 0)),
        compiler_params=_params("arbitrary"),
        name="norm_mod",
    )(x, g, sc, sh)


def _resid_kernel(x_ref, y_ref, gpost_ref, gt_ref, *rest, with_h):
    xn = x_ref[...] + gt_ref[...] * (_rms(y_ref[...]) * gpost_ref[...])
    if with_h:
        gpre_ref, sc_ref, sh_ref, xo_ref, ho_ref = rest
        xo_ref[...] = xn
        h = _rms(xn) * gpre_ref[...]
        ho_ref[...] = (h * (1.0 + sc_ref[...]) + sh_ref[...]).astype(ho_ref.dtype)
    else:
        (xo_ref,) = rest
        xo_ref[...] = xn


def _resid(x, y, gpost, gt, pre=None):
    m, d = x.shape
    tm = _pick(m, (256, 128, 64, 32, 16))
    vec = pl.BlockSpec((1, d), lambda i: (0, 0))
    row = pl.BlockSpec((tm, d), lambda i: (i, 0))
    with_h = pre is not None
    out_shape = [jax.ShapeDtypeStruct((m, d), F32)]
    out_specs = [row]
    args = [x, y, gpost, gt]
    in_specs = [row, row, vec, vec]
    if with_h:
        out_shape.append(jax.ShapeDtypeStruct((m, d), BF16))
        out_specs.append(row)
        args += list(pre)
        in_specs += [vec, vec, vec]
    out = pl.pallas_call(
        functools.partial(_resid_kernel, with_h=with_h),
        out_shape=out_shape,
        grid=(m // tm,),
        in_specs=in_specs,
        out_specs=out_specs,
        compiler_params=_params("arbitrary"),
        name="resid",
    )(*args)
    return (out[0], out[1]) if with_h else (out[0], None)


def _mm_kernel(a_ref, w_ref, o_ref):
    o_ref[...] = _dot(a_ref[...], w_ref[...]).astype(o_ref.dtype)


def _mm(a, w, out_dtype):
    m, k = a.shape
    n = w.shape[1]
    tm = _pick(m, (1024, 1280, 512, 256, 128))
    tn = _pick(n, (1024, 512, 256, 128))
    return pl.pallas_call(
        _mm_kernel,
        out_shape=jax.ShapeDtypeStruct((m, n), out_dtype),
        grid=(m // tm, n // tn),
        in_specs=[pl.BlockSpec((tm, k), lambda i, j: (i, 0)),
                  pl.BlockSpec((k, tn), lambda i, j: (0, j))],
        out_specs=pl.BlockSpec((tm, tn), lambda i, j: (i, j)),
        compiler_params=_params("arbitrary", "arbitrary"),
        name="mm",
    )(a, w)


def _mm2_kernel(a1_ref, w1_ref, a2_ref, w2_ref, o_ref):
    o_ref[...] = (_dot(a1_ref[...], w1_ref[...]) + _dot(a2_ref[...], w2_ref[...])).astype(o_ref.dtype)


def _mm2(a1, w1, a2, w2, out_dtype):
    m, k1 = a1.shape
    k2 = a2.shape[1]
    n = w1.shape[1]
    tm = _pick(m, (1024, 512, 256, 128))
    tn = _pick(n, (1024, 512, 256, 128))
    return pl.pallas_call(
        _mm2_kernel,
        out_shape=jax.ShapeDtypeStruct((m, n), out_dtype),
        grid=(m // tm, n // tn),
        in_specs=[pl.BlockSpec((tm, k1), lambda i, j: (i, 0)),
                  pl.BlockSpec((k1, tn), lambda i, j: (0, j)),
                  pl.BlockSpec((tm, k2), lambda i, j: (i, 0)),
                  pl.BlockSpec((k2, tn), lambda i, j: (0, j))],
        out_specs=pl.BlockSpec((tm, tn), lambda i, j: (i, j)),
        compiler_params=_params("arbitrary", "arbitrary"),
        name="mm2",
    )(a1, w1, a2, w2)


def _na_bias(rpb, rows):
    w = GRID_W
    h = rpb.shape[0]
    qc = np.arange(w)[:, None]
    kc = np.arange(w)[None, :]
    c0 = np.clip(qc - NA_WIN_C // 2, 0, w - NA_WIN_C)
    ok_c = (kc >= c0) & (kc < c0 + NA_WIN_C)
    dc = np.clip(kc - qc + NA_WIN_C - 1, 0, 2 * NA_WIN_C - 2)
    onehot = (dc[None] == np.arange(2 * NA_WIN_C - 1)[:, None, None]).astype(np.float32)
    t = jnp.einsum('hdj,jqk->hdqk', rpb.astype(F32), onehot, precision=lax.Precision.HIGHEST)
    t = jnp.where(ok_c, t, NEG)
    masked = jnp.full((h, w, w), NEG, F32)
    out = []
    for r_first, k_first in ((0, 0), (NA_Q_ROWS, 0), (rows - NA_Q_ROWS, rows - NA_K_ROWS)):
        q_rows = []
        for qr in range(NA_Q_ROWS):
            r = r_first + qr
            r0 = min(max(r - NA_WIN_R // 2, 0), rows - NA_WIN_R)
            blocks = []
            for kr in range(NA_K_ROWS):
                ka = k_first + kr
                blocks.append(t[:, ka - r + NA_WIN_R - 1] if r0 <= ka < r0 + NA_WIN_R else masked)
            q_rows.append(jnp.concatenate(blocks, axis=2))
        out.append(jnp.concatenate(q_rows, axis=1))
    return jnp.stack(out)


def _na_kernel(q_ref, k0_ref, k1_ref, k2_ref, v0_ref, v1_ref, v2_ref, kc_ref, vc_ref, b_ref, o_ref,
               *, scale):
    q = q_ref[...]
    s_loc = jnp.concatenate([_dot_nt(q, k_ref[...]) for k_ref in (k0_ref, k1_ref, k2_ref)], axis=1)
    s_loc = s_loc * scale + b_ref[...]
    s_ctx = _dot_nt(q, kc_ref[...]) * scale
    m = jnp.maximum(jnp.max(s_loc, axis=1, keepdims=True), jnp.max(s_ctx, axis=1, keepdims=True))
    p_loc = jnp.exp(s_loc - m)
    p_ctx = jnp.exp(s_ctx - m)
    denom = jnp.sum(p_loc, axis=1, keepdims=True) + jnp.sum(p_ctx, axis=1, keepdims=True)
    qb = q.shape[0]
    o = _dot(p_ctx.astype(BF16), vc_ref[...])
    for j, v_ref in enumerate((v0_ref, v1_ref, v2_ref)):
        o = o + _dot(p_loc[:, j * qb:(j + 1) * qb].astype(BF16), v_ref[...])
    o_ref[...] = (o / denom).astype(o_ref.dtype)


def _na_attention(qkv_l, qkv_c, bias):
    n = qkv_l.shape[0]
    lc = qkv_c.shape[0]
    width = qkv_l.shape[1] // 3
    heads = width // NA_HEAD_DIM
    qb = NA_Q_ROWS * GRID_W
    nblk = n // qb
    assert n % qb == 0 and nblk >= 3

    def kv_spec(j, col0):
        return pl.BlockSpec(
            (qb, NA_HEAD_DIM),
            lambda h, i: (jnp.clip(i - 1, 0, nblk - 3) + j, col0 + h))

    def bias_map(h, i):
        return (jnp.where(i == 0, 0, jnp.where(i == nblk - 1, 2, 1)), h, 0, 0)

    return pl.pallas_call(
        functools.partial(_na_kernel, scale=NA_HEAD_DIM ** -0.5),
        out_shape=jax.ShapeDtypeStruct((n, width), BF16),
        grid=(heads, nblk),
        in_specs=[
            pl.BlockSpec((qb, NA_HEAD_DIM), lambda h, i: (i, h)),
            kv_spec(0, heads), kv_spec(1, heads), kv_spec(2, heads),
            kv_spec(0, 2 * heads), kv_spec(1, 2 * heads), kv_spec(2, 2 * heads),
            pl.BlockSpec((lc, NA_HEAD_DIM), lambda h, i: (0, heads + h)),
            pl.BlockSpec((lc, NA_HEAD_DIM), lambda h, i: (0, 2 * heads + h)),
            pl.BlockSpec((None, None, qb, 3 * qb), bias_map),
        ],
        out_specs=pl.BlockSpec((qb, NA_HEAD_DIM), lambda h, i: (i, h)),
        compiler_params=_params("arbitrary", "arbitrary"),
        name="na_attention",
    )(qkv_l, qkv_l, qkv_l, qkv_l, qkv_l, qkv_l, qkv_l, qkv_c, qkv_c, bias)


def _dense_attn_kernel(q_ref, k_ref, v_ref, o_ref, *, scale):
    s = _dot_nt(q_ref[...], k_ref[...]) * scale
    p = jnp.exp(s - jnp.max(s, axis=1, keepdims=True))
    o = _dot(p.astype(BF16), v_ref[...])
    o_ref[...] = (o / jnp.sum(p, axis=1, keepdims=True)).astype(o_ref.dtype)


def _dense_attention(qkv):
    n = qkv.shape[0]
    width = qkv.shape[1] // 3
    heads = width // NA_HEAD_DIM
    return pl.pallas_call(
        functools.partial(_dense_attn_kernel, scale=NA_HEAD_DIM ** -0.5),
        out_shape=jax.ShapeDtypeStruct((n, width), BF16),
        grid=(heads,),
        in_specs=[pl.BlockSpec((n, NA_HEAD_DIM), lambda h: (0, h)),
                  pl.BlockSpec((n, NA_HEAD_DIM), lambda h: (0, heads + h)),
                  pl.BlockSpec((n, NA_HEAD_DIM), lambda h: (0, 2 * heads + h))],
        out_specs=pl.BlockSpec((n, NA_HEAD_DIM), lambda h: (0, h)),
        compiler_params=_params("arbitrary"),
        name="dense_attention",
    )(qkv, qkv, qkv)


def _pool_kernel(u_ref, o_ref, *, n, chunk, blocks_per_group):
    group = pl.program_id(0) // blocks_per_group
    rows = chunk + 2 * POOL_HALO

    def window_sum(xs, w):
        a = pltpu.roll(xs, 1, axis=0) + xs
        half = 1
        while 2 * half < w:
            a = pltpu.roll(a, half, axis=0) + pltpu.roll(a, rows - half, axis=0)
            half *= 2
        return a

    for gi, w in enumerate(POOL_WINDOWS):
        @pl.when(group == gi)
        def _(w=w):
            def body(c, carry):
                base = pl.multiple_of(c * chunk, chunk)
                xs = u_ref[pl.ds(base, rows), :]
                tot = window_sum(xs, w)[POOL_HALO:POOL_HALO + chunk]
                t = base + lax.broadcasted_iota(jnp.int32, (chunk, 1), 0)
                cnt = jnp.minimum(t + w // 2, n) - jnp.maximum(t - w // 2, 0)
                mean = tot / cnt.astype(F32)
                o_ref[pl.ds(base, chunk), :] = (mean - xs[POOL_HALO:POOL_HALO + chunk]).astype(o_ref.dtype)
                return carry
            lax.fori_loop(0, n // chunk, body, 0)


def _pool_diff(u):
    n, c = u.shape
    cg = c // len(POOL_WINDOWS)
    assert cg % LANES == 0
    chunk = _pick(n, (512, 256, 128))
    up = jnp.pad(u, ((POOL_HALO, POOL_HALO), (0, 0)))
    return pl.pallas_call(
        functools.partial(_pool_kernel, n=n, chunk=chunk, blocks_per_group=cg // LANES),
        out_shape=jax.ShapeDtypeStruct((n, c), BF16),
        grid=(c // LANES,),
        in_specs=[pl.BlockSpec((n + 2 * POOL_HALO, LANES), lambda j: (0, j))],
        out_specs=pl.BlockSpec((n, LANES), lambda j: (0, j)),
        compiler_params=_params("arbitrary"),
        name="pool_diff",
    )(up)


def _group_mm_kernel(a_ref, w_ref, s_ref, o_ref):
    o_ref[...] = (_dot(a_ref[...], w_ref[...]) * s_ref[...]).astype(o_ref.dtype)


def _group_mm(d, w_grp, ch_scale):
    n, c = d.shape
    g, cg, _ = w_grp.shape
    tm = _pick(n, (1024, 512, 256, 128))
    return pl.pallas_call(
        _group_mm_kernel,
        out_shape=jax.ShapeDtypeStruct((n, c), BF16),
        grid=(g, n // tm),
        in_specs=[pl.BlockSpec((tm, cg), lambda gi, i: (i, gi)),
                  pl.BlockSpec((None, cg, cg), lambda gi, i: (gi, 0, 0)),
                  pl.BlockSpec((1, cg), lambda gi, i: (0, gi))],
        out_specs=pl.BlockSpec((tm, cg), lambda gi, i: (i, gi)),
        compiler_params=_params("arbitrary", "arbitrary"),
        name="group_mm",
    )(d, w_grp, ch_scale)


def _rope_tables(n):
    nf = MLA_ROPE // 4
    t = jnp.arange(n)
    row = (t // GRID_W).astype(F32)
    col = (t % GRID_W).astype(F32)
    freqs = ROPE_THETA ** (-jnp.arange(nf, dtype=F32) / nf)
    ang = jnp.stack([row[:, None] * freqs, col[:, None] * freqs], axis=1)
    cos, sin = jnp.cos(ang), jnp.sin(ang)
    c = jnp.stack([cos, cos], axis=2).reshape(n, MLA_ROPE)
    s = jnp.stack([-sin, sin], axis=2).reshape(n, MLA_ROPE)
    pad = jnp.zeros((n, LANES - MLA_ROPE), F32)
    return jnp.concatenate([c, pad], axis=1), jnp.concatenate([s, pad], axis=1)


def _rope_cols(w):
    nf = MLA_ROPE // 4
    j = jnp.arange(MLA_ROPE)
    partner = (j // (2 * nf)) * (2 * nf) + (1 - (j // nf) % 2) * nf + j % nf
    z = jnp.zeros((w.shape[0], LANES - MLA_ROPE), w.dtype)
    return jnp.concatenate([w, z, w[:, partner], z], axis=1)


def _mla_down_kernel(h_ref, w_ref, gq_ref, gkv_ref, c_ref, s_ref, qn_ref, ckv_ref, kr_ref, *, qr, kvr):
    z = _dot(h_ref[...], w_ref[...])
    qn_ref[...] = (_rms(z[:, :qr]) * gq_ref[...]).astype(qn_ref.dtype)
    ckv_ref[...] = (_rms(z[:, qr:qr + kvr]) * gkv_ref[...]).astype(ckv_ref.dtype)
    r0 = qr + kvr
    rot = z[:, r0:r0 + LANES] * c_ref[...] + z[:, r0 + LANES:r0 + 2 * LANES] * s_ref[...]
    kr_ref[...] = rot.astype(kr_ref.dtype)


def _mla_down(h, w_ext, g_q, g_kv, ctab, stab):
    n, d = h.shape
    qr = g_q.shape[1]
    kvr = g_kv.shape[1]
    nw = w_ext.shape[1]
    tm = _pick(n, (512, 256, 128))
    return pl.pallas_call(
        functools.partial(_mla_down_kernel, qr=qr, kvr=kvr),
        out_shape=[jax.ShapeDtypeStruct((n, qr), BF16),
                   jax.ShapeDtypeStruct((n, kvr), BF16),
                   jax.ShapeDtypeStruct((n, LANES), BF16)],
        grid=(n // tm,),
        in_specs=[pl.BlockSpec((tm, d), lambda i: (i, 0)),
                  pl.BlockSpec((d, nw), lambda i: (0, 0)),
                  pl.BlockSpec((1, qr), lambda i: (0, 0)),
                  pl.BlockSpec((1, kvr), lambda i: (0, 0)),
                  pl.BlockSpec((tm, LANES), lambda i: (i, 0)),
                  pl.BlockSpec((tm, LANES), lambda i: (i, 0))],
        out_specs=[pl.BlockSpec((tm, qr), lambda i: (i, 0)),
                   pl.BlockSpec((tm, kvr), lambda i: (i, 0)),
                   pl.BlockSpec((tm, LANES), lambda i: (i, 0))],
        compiler_params=_params("arbitrary"),
        name="mla_down",
    )(h, w_ext, g_q, g_kv, ctab, stab)


MLA_Q_IN = MLA_NOPE + 2 * LANES
MLA_Q_OUT = MLA_NOPE + LANES


def _q_up_kernel(a_ref, w_ref, c_ref, s_ref, o_ref, *, heads, scale):
    z = _dot(a_ref[...], w_ref[...])
    c = c_ref[...] * scale
    s = s_ref[...] * scale
    for j in range(heads):
        zi = j * MLA_Q_IN
        oi = j * MLA_Q_OUT
        o_ref[:, oi:oi + MLA_NOPE] = (z[:, zi:zi + MLA_NOPE] * scale).astype(o_ref.dtype)
        rot = z[:, zi + MLA_NOPE:zi + MLA_NOPE + LANES] * c + z[:, zi + MLA_NOPE + LANES:zi + MLA_Q_IN] * s
        o_ref[:, oi + MLA_NOPE:oi + MLA_Q_OUT] = rot.astype(o_ref.dtype)


def _q_up(qn, w_ext, ctab, stab, heads):
    n, qr = qn.shape
    tm = _pick(n, (1024, 512, 256, 128))
    hb = _pick(heads, (4, 2, 1))
    scale = (MLA_NOPE + MLA_ROPE) ** -0.5 * LOG2E
    return pl.pallas_call(
        functools.partial(_q_up_kernel, heads=hb, scale=scale),
        out_shape=jax.ShapeDtypeStruct((n, heads * MLA_Q_OUT), BF16),
        grid=(n // tm, heads // hb),
        in_specs=[pl.BlockSpec((tm, qr), lambda i, j: (i, 0)),
                  pl.BlockSpec((qr, hb * MLA_Q_IN), lambda i, j: (0, j)),
                  pl.BlockSpec((tm, LANES), lambda i, j: (i, 0)),
                  pl.BlockSpec((tm, LANES), lambda i, j: (i, 0))],
        out_specs=pl.BlockSpec((tm, hb * MLA_Q_OUT), lambda i, j: (i, j)),
        compiler_params=_params("arbitrary", "arbitrary"),
        name="mla_q_up",
    )(qn, w_ext, ctab, stab)


def _mla_attn_kernel(q_ref, kv_ref, kr_ref, *rest, tk, sub, has_ctx):
    if has_ctx:
        kvc_ref, krc_ref, o_ref, qt_sc, vt_sc, vtc_sc, st0_sc, st1_sc, m_sc, l_sc, acc_sc = rest
    else:
        o_ref, qt_sc, vt_sc, st0_sc, st1_sc, m_sc, l_sc, acc_sc = rest
    tq = q_ref.shape[0]
    nchunks = kv_ref.shape[0] // tk
    cols = [slice(j * sub, (j + 1) * sub) for j in range(tq // sub)]

    @pl.when(pl.program_id(1) == 0)
    def _():
        def tr(c, carry):
            ks = pl.multiple_of(c * tk, tk)
            vt_sc[c] = kv_ref[pl.ds(ks, tk), MLA_NOPE:MLA_NOPE + MLA_V].T
            return carry
        lax.fori_loop(0, nchunks, tr, 0)
        if has_ctx:
            vtc_sc[...] = kvc_ref[:, MLA_NOPE:MLA_NOPE + MLA_V].T

    qt_sc[...] = q_ref[...].T
    m_sc[...] = jnp.full(m_sc.shape, NEG, F32)
    l_sc[...] = jnp.zeros(l_sc.shape, F32)
    acc_sc[...] = jnp.zeros(acc_sc.shape, F32)

    def keys(c):
        ks = pl.multiple_of(c * tk, tk)
        return jnp.concatenate([kv_ref[pl.ds(ks, tk), 0:MLA_NOPE], kr_ref[pl.ds(ks, tk), :]], axis=1)

    def scores(k):
        return [_dot(k, qt_sc[:, c]) for c in cols]

    def softmax_pv(sts, vt):
        m_prev = m_sc[...]
        m_new = jnp.maximum(m_prev, jnp.concatenate(
            [jnp.max(st(), axis=0, keepdims=True) for st in sts], axis=1))
        alpha = jnp.exp2(m_prev - m_new)
        sums, pvs = [], []
        for st, c in zip(sts, cols):
            p = jnp.exp2(st() - m_new[:, c])
            sums.append(jnp.sum(p, axis=0, keepdims=True))
            pvs.append(_dot(vt, p.astype(BF16)))
        l_sc[...] = alpha * l_sc[...] + jnp.concatenate(sums, axis=1)
        acc_sc[...] = alpha * acc_sc[...] + jnp.concatenate(pvs, axis=1)
        m_sc[...] = m_new

    def stage(k, st_sc):
        for c, st in zip(cols, scores(k)):
            st_sc[:, c] = st

    def staged(st_sc):
        return [functools.partial(lambda c: st_sc[:, c], c) for c in cols]

    if has_ctx:
        kc = jnp.concatenate([kvc_ref[:, 0:MLA_NOPE], krc_ref[...]], axis=1)
        softmax_pv([functools.partial(lambda s: s, s) for s in scores(kc)], vtc_sc[...])
    if nchunks == 1:
        softmax_pv([functools.partial(lambda s: s, s) for s in scores(keys(0))], vt_sc[0])
    else:
        assert nchunks % 2 == 0
        stage(keys(0), st0_sc)

        def body(i, carry):
            stage(keys(2 * i + 1), st1_sc)
            softmax_pv(staged(st0_sc), vt_sc[2 * i])
            stage(keys(jnp.minimum(2 * i + 2, nchunks - 1)), st0_sc)
            softmax_pv(staged(st1_sc), vt_sc[2 * i + 1])
            return carry

        lax.fori_loop(0, nchunks // 2, body, 0)
    o_ref[...] = (acc_sc[...] / l_sc[...]).T.astype(o_ref.dtype)


def _mla_attention(q, kv, kr, kv_c=None, kr_c=None):
    nq = q.shape[0]
    s = kv.shape[0]
    heads = q.shape[1] // MLA_Q_OUT
    tq = _pick(nq, (1024, 512, 256, 128))
    sub = _pick(tq, (256, 128))
    tk = _pick(s, (512, 256, 128))
    has_ctx = kv_c is not None
    kvw = MLA_NOPE + MLA_V
    in_specs = [pl.BlockSpec((tq, MLA_Q_OUT), lambda h, i: (i, h)),
                pl.BlockSpec((s, kvw), lambda h, i: (0, h)),
                pl.BlockSpec((s, LANES), lambda h, i: (0, 0))]
    args = [q, kv, kr]
    scratch = [pltpu.VMEM((MLA_Q_OUT, tq), BF16), pltpu.VMEM((s // tk, MLA_V, tk), BF16)]
    if has_ctx:
        sc = kv_c.shape[0]
        in_specs += [pl.BlockSpec((sc, kvw), lambda h, i: (0, h)),
                     pl.BlockSpec((sc, LANES), lambda h, i: (0, 0))]
        args += [kv_c, kr_c]
        scratch.append(pltpu.VMEM((MLA_V, sc), BF16))
    scratch += [pltpu.VMEM((tk, tq), F32), pltpu.VMEM((tk, tq), F32)]
    scratch += [pltpu.VMEM((1, tq), F32), pltpu.VMEM((1, tq), F32), pltpu.VMEM((MLA_V, tq), F32)]
    return pl.pallas_call(
        functools.partial(_mla_attn_kernel, tk=tk, sub=sub, has_ctx=has_ctx),
        out_shape=jax.ShapeDtypeStruct((nq, heads * MLA_V), BF16),
        grid=(heads, nq // tq),
        in_specs=in_specs,
        out_specs=pl.BlockSpec((tq, MLA_V), lambda h, i: (i, h)),
        scratch_shapes=scratch,
        compiler_params=_params("arbitrary", "arbitrary"),
        name="mla_attention",
    )(*args)


GLU_HALO = 16


def _glu_down_kernel(g_ref, v_ref, gp_ref, gn_ref, cw_ref, cb_ref, w_ref, o_ref):
    i = pl.program_id(0)
    k = pl.program_id(1)
    tm = g_ref.shape[0]
    g = g_ref[...].astype(F32)
    prev_row = jnp.where(i > 0, gp_ref[GLU_HALO - 1:GLU_HALO, :].astype(F32), 0.0)
    next_row = jnp.where(i < pl.num_programs(0) - 1, gn_ref[0:1, :].astype(F32), 0.0)
    row = lax.broadcasted_iota(jnp.int32, (tm, 1), 0)
    g_dn = jnp.where(row == 0, prev_row, pltpu.roll(g, 1, axis=0))
    g_up = jnp.where(row == tm - 1, next_row, pltpu.roll(g, tm - 1, axis=0))
    cw = cw_ref[...]
    gc = g_dn * cw[0:1, :] + g * cw[1:2, :] + g_up * cw[2:3, :] + cb_ref[...]
    a = (gc * jax.nn.sigmoid(gc)) * v_ref[...].astype(F32)
    part = _dot(a.astype(BF16), w_ref[...])

    @pl.when(k == 0)
    def _():
        o_ref[...] = part

    @pl.when(k > 0)
    def _():
        o_ref[...] += part


def _glu_down(u, conv_w, conv_b, w_down):
    n = u.shape[0]
    dff, d = w_down.shape
    tm = _pick(n, (512, 256, 128))
    tk = _pick(dff, (1024, 512, 256, 128))
    nk = dff // tk
    hb = tm // GLU_HALO
    nhb = n // GLU_HALO
    return pl.pallas_call(
        _glu_down_kernel,
        out_shape=jax.ShapeDtypeStruct((n, d), F32),
        grid=(n // tm, nk),
        in_specs=[pl.BlockSpec((tm, tk), lambda i, k: (i, k)),
                  pl.BlockSpec((tm, tk), lambda i, k: (i, nk + k)),
                  pl.BlockSpec((GLU_HALO, tk), lambda i, k: (jnp.maximum(i * hb - 1, 0), k)),
                  pl.BlockSpec((GLU_HALO, tk), lambda i, k: (jnp.minimum((i + 1) * hb, nhb - 1), k)),
                  pl.BlockSpec((3, tk), lambda i, k: (0, k)),
                  pl.BlockSpec((1, tk), lambda i, k: (0, k)),
                  pl.BlockSpec((tk, d), lambda i, k: (k, 0))],
        out_specs=pl.BlockSpec((tm, d), lambda i, k: (i, 0)),
        compiler_params=_params("arbitrary", "arbitrary"),
        name="glu_down",
    )(u, u, u, u, conv_w, conv_b, w_down)


def _conv_glu(h, w_up, conv_w, conv_b, w_down):
    return _glu_down(_mm(h, w_up, BF16), conv_w, conv_b, w_down)


def _ab_mixer(hl, hc, w_in, rpb, pool_w, pool_scale, w_out, need_ctx):
    n = hl.shape[0]
    naw = w_out.shape[0] // 2
    w_qkv = w_in[:, :3 * naw].astype(BF16)
    w_u = w_in[:, 3 * naw:].astype(BF16)
    w_out_a = w_out[:naw].astype(BF16)
    w_out_b = w_out[naw:].astype(BF16)
    pool_w = pool_w.astype(BF16)
    pool_scale = pool_scale.reshape(1, -1)
    qkv_l = _mm(hl, w_qkv, BF16)
    qkv_c = _mm(hc, w_qkv, BF16)
    o_a = _na_attention(qkv_l, qkv_c, _na_bias(rpb, n // GRID_W))
    o_b = _group_mm(_pool_diff(_mm(hl, w_u, F32)), pool_w, pool_scale)
    yl = _mm2(o_a, w_out_a, o_b, w_out_b, F32)
    yc = None
    if need_ctx:
        o_ac = _dense_attention(qkv_c)
        o_bc = _group_mm(_pool_diff(_mm(hc, w_u, F32)), pool_w, pool_scale)
        yc = _mm2(o_ac, w_out_a, o_bc, w_out_b, F32)
    return yl, yc


def _mla_mixer(hl, hc, w_down, g_q, g_kv, w_uq, w_ukv, w_out, tabs_l, tabs_c, need_ctx):
    qr = g_q.shape[0]
    kvr = g_kv.shape[0]
    heads = w_ukv.shape[1] // (MLA_NOPE + MLA_V)
    w_down_ext = jnp.concatenate([w_down[:, :qr + kvr], _rope_cols(w_down[:, qr + kvr:])], axis=1).astype(BF16)
    w_uq_h = w_uq.reshape(qr, heads, MLA_NOPE + MLA_ROPE)
    w_uq_ext = jnp.concatenate(
        [w_uq_h[..., :MLA_NOPE],
         _rope_cols(w_uq_h[..., MLA_NOPE:].reshape(qr * heads, MLA_ROPE)).reshape(qr, heads, 2 * LANES)],
        axis=-1).reshape(qr, heads * MLA_Q_IN).astype(BF16)
    w_ukv = w_ukv.astype(BF16)
    w_out = w_out.astype(BF16)
    g_q = g_q.reshape(1, -1)
    g_kv = g_kv.reshape(1, -1)
    qn_l, ckv_l, kr_l = _mla_down(hl, w_down_ext, g_q, g_kv, *tabs_l)
    qn_c, ckv_c, kr_c = _mla_down(hc, w_down_ext, g_q, g_kv, *tabs_c)
    q_l = _q_up(qn_l, w_uq_ext, *tabs_l, heads)
    kv_l = _mm(ckv_l, w_ukv, BF16)
    kv_c = _mm(ckv_c, w_ukv, BF16)
    yl = _mm(_mla_attention(q_l, kv_l, kr_l, kv_c, kr_c), w_out, F32)
    yc = None
    if need_ctx:
        q_c = _q_up(qn_c, w_uq_ext, *tabs_c, heads)
        yc = _mm(_mla_attention(q_c, kv_c, kr_c), w_out, F32)
    return yl, yc


def kernel(x, c, ctx, c_ctx, w_ada, b_ada, g_mix_pre, g_mix_post, g_ffn_pre, g_ffn_post, ab_w_in, na_rpb, pool_w, pool_scale, ab_w_out, mla_w_down, mla_g_q, mla_g_kv, mla_w_uq, mla_w_ukv, mla_w_out, ffn_w_up, ffn_conv_w, ffn_conv_b, ffn_w_down):
    batch, n, d = x.shape
    lc = ctx.shape[1]
    depth = w_ada.shape[0]
    assert batch == 1 and c.shape[0] == 1
    xl = x.reshape(n, d)
    xc = ctx.reshape(lc, d)

    cvec = jnp.concatenate([c, c_ctx[None, :], jnp.zeros((6, d), F32)], axis=0)
    mod = _ada(cvec, w_ada, b_ada)

    def mods(l, r):
        return [mod[l, r:r + 1, j * d:(j + 1) * d] for j in range(6)]

    tabs_l = _rope_tables(n)
    tabs_c = (jnp.concatenate([jnp.ones((lc, MLA_ROPE), F32), jnp.zeros((lc, LANES - MLA_ROPE), F32)], axis=1),
              jnp.zeros((lc, LANES), F32))

    vec = lambda a: a.reshape(1, -1)
    sh1, sc1, _, _, _, _ = mods(0, 0)
    sh1c, sc1c, _, _, _, _ = mods(0, 1)
    hl = _norm_mod(xl, vec(g_mix_pre[0]), sc1, sh1)
    hc = _norm_mod(xc, vec(g_mix_pre[0]), sc1c, sh1c)
    for l in range(depth):
        last = l == depth - 1
        sh1, sc1, gt1, sh2, sc2, gt2 = mods(l, 0)
        sh1c, sc1c, gt1c, sh2c, sc2c, gt2c = mods(l, 1)
        if l % 2 == 0:
            e = l // 2
            yl, yc = _ab_mixer(hl, hc, ab_w_in[e], na_rpb[e], pool_w[e], pool_scale[e], ab_w_out[e], not last)
        else:
            o = l // 2
            yl, yc = _mla_mixer(hl, hc, mla_w_down[o], mla_g_q[o], mla_g_kv[o], mla_w_uq[o], mla_w_ukv[o],
                                mla_w_out[o], tabs_l, tabs_c, not last)
        w_up = ffn_w_up[l].astype(BF16)
        w_dn = ffn_w_down[l].astype(BF16)
        conv_b = vec(ffn_conv_b[l])
        xl, h2 = _resid(xl, yl, vec(g_mix_post[l]), gt1, (vec(g_ffn_pre[l]), sc2, sh2))
        f = _conv_glu(h2, w_up, ffn_conv_w[l], conv_b, w_dn)
        if last:
            xl, _ = _resid(xl, f, vec(g_ffn_post[l]), gt2)
        else:
            nsh1, nsc1, _, _, _, _ = mods(l + 1, 0)
            xl, hl = _resid(xl, f, vec(g_ffn_post[l]), gt2, (vec(g_mix_pre[l + 1]), nsc1, nsh1))
            xc, h2c = _resid(xc, yc, vec(g_mix_post[l]), gt1c, (vec(g_ffn_pre[l]), sc2c, sh2c))
            fc = _conv_glu(h2c, w_up, ffn_conv_w[l], conv_b, w_dn)
            nsh1c, nsc1c, _, _, _, _ = mods(l + 1, 1)
            xc, hc = _resid(xc, fc, vec(g_ffn_post[l]), gt2c, (vec(g_mix_pre[l + 1]), nsc1c, nsh1c))
    return xl.reshape(batch, n, d)
```

```python
import functools

import numpy as np
import jax
import jax.numpy as jnp
from jax import lax
from jax.experimental import pallas as pl
from jax.experimental.pallas import tpu as pltpu

GRID_W = 64
EPS = 1e-6
NA_HEAD_DIM = 128
NA_WIN_R = 8
NA_WIN_C = 16
NA_Q_ROWS = 4
NA_K_ROWS = NA_Q_ROWS + NA_WIN_R
POOL_WINDOWS = (2, 4, 8, 16)
POOL_HALO = 8
MLA_NOPE = 128
MLA_ROPE = 64
MLA_V = 128
ROPE_THETA = 10000.0
LANES = 128
NEG = -1e30
LOG2E = 1.4426950408889634
VMEM_LIMIT = 56 * 1024 * 1024

F32 = jnp.float32
BF16 = jnp.bfloat16


def _pick(n, prefs):
    for p in prefs:
        if n % p == 0:
            return p
    return n


def _params(*sem):
    return pltpu.CompilerParams(dimension_semantics=sem, vmem_limit_bytes=VMEM_LIMIT)


def _rms(x):
    return x * lax.rsqrt(jnp.mean(x * x, axis=-1, keepdims=True) + EPS)


def _dot(a, b):
    return jnp.dot(a, b, preferred_element_type=F32)


def _dot_nt(a, b):
    return lax.dot_general(a, b, (((1,), (1,)), ((), ())), preferred_element_type=F32)


def _ada_kernel(c_ref, w_ref, b_ref, o_ref):
    c = c_ref[...]
    s = (c * jax.nn.sigmoid(c)).astype(BF16)
    o_ref[...] = _dot(s, w_ref[...].astype(BF16)) + b_ref[...]


def _ada(cvec, w_ada, b_ada):
    depth, d, n6 = w_ada.shape
    rows = cvec.shape[0]
    tn = _pick(n6, (512, 256, 128))
    return pl.pallas_call(
        _ada_kernel,
        out_shape=jax.ShapeDtypeStruct((depth, rows, n6), F32),
        grid=(depth, n6 // tn),
        in_specs=[
            pl.BlockSpec((rows, d), lambda l, j: (0, 0)),
            pl.BlockSpec((None, d, tn), lambda l, j: (l, 0, j)),
            pl.BlockSpec((None, 1, tn), lambda l, j: (l, 0, j)),
        ],
        out_specs=pl.BlockSpec((None, rows, tn), lambda l, j: (l, 0, j)),
        compiler_params=_params("arbitrary", "arbitrary"),
        name="ada",
    )(cvec, w_ada, b_ada.reshape(depth, 1, n6))


def _norm_mod_kernel(x_ref, g_ref, sc_ref, sh_ref, o_ref):
    y = _rms(x_ref[...]) * g_ref[...]
    o_ref[...] = (y * (1.0 + sc_ref[...]) + sh_ref[...]).astype(o_ref.dtype)


def _norm_mod(x, g, sc, sh):
    m, d = x.shape
    tm = _pick(m, (256, 128, 64, 32, 16))
    vec = pl.BlockSpec((1, d), lambda i: (0, 0))
    return pl.pallas_call(
        _norm_mod_kernel,
        out_shape=jax.ShapeDtypeStruct((m, d), BF16),
        grid=(m // tm,),
        in_specs=[pl.BlockSpec((tm, d), lambda i: (i, 0)), vec, vec, vec],
        out_specs=pl.BlockSpec((tm, d), lambda i: (i, 0)),
        compiler_params=_params("arbitrary"),
        name="norm_mod",
    )(x, g, sc, sh)


def _resid_kernel(x_ref, y_ref, gpost_ref, gt_ref, *rest, with_h):
    xn = x_ref[...] + gt_ref[...] * (_rms(y_ref[...]) * gpost_ref[...])
    if with_h:
        gpre_ref, sc_ref, sh_ref, xo_ref, ho_ref = rest
        xo_ref[...] = xn
        h = _rms(xn) * gpre_ref[...]
        ho_ref[...] = (h * (1.0 + sc_ref[...]) + sh_ref[...]).astype(ho_ref.dtype)
    else:
        (xo_ref,) = rest
        xo_ref[...] = xn


def _resid(x, y, gpost, gt, pre=None):
    m, d = x.shape
    tm = _pick(m, (256, 128, 64, 32, 16))
    vec = pl.BlockSpec((1, d), lambda i: (0, 0))
    row = pl.BlockSpec((tm, d), lambda i: (i, 0))
    with_h = pre is not None
    out_shape = [jax.ShapeDtypeStruct((m, d), F32)]
    out_specs = [row]
    args = [x, y, gpost, gt]
    in_specs = [row, row, vec, vec]
    if with_h:
        out_shape.append(jax.ShapeDtypeStruct((m, d), BF16))
        out_specs.append(row)
        args += list(pre)
        in_specs += [vec, vec, vec]
    out = pl.pallas_call(
        functools.partial(_resid_kernel, with_h=with_h),
        out_shape=out_shape,
        grid=(m // tm,),
        in_specs=in_specs,
        out_specs=out_specs,
        compiler_params=_params("arbitrary"),
        name="resid",
    )(*args)
    return (out[0], out[1]) if with_h else (out[0], None)


def _mm_kernel(a_ref, w_ref, o_ref):
    o_ref[...] = _dot(a_ref[...], w_ref[...]).astype(o_ref.dtype)


def _mm(a, w, out_dtype):
    m, k = a.shape
    n = w.shape[1]
    tm = _pick(m, (1024, 1280, 512, 256, 128))
    tn = _pick(n, (1024, 512, 256, 128))
    return pl.pallas_call(
        _mm_kernel,
        out_shape=jax.ShapeDtypeStruct((m, n), out_dtype),
        grid=(m // tm, n // tn),
        in_specs=[pl.BlockSpec((tm, k), lambda i, j: (i, 0)),
                  pl.BlockSpec((k, tn), lambda i, j: (0, j))],
        out_specs=pl.BlockSpec((tm, tn), lambda i, j: (i, j)),
        compiler_params=_params("arbitrary", "arbitrary"),
        name="mm",
    )(a, w)


def _mm2_kernel(a1_ref, w1_ref, a2_ref, w2_ref, o_ref):
    o_ref[...] = (_dot(a1_ref[...], w1_ref[...]) + _dot(a2_ref[...], w2_ref[...])).astype(o_ref.dtype)


def _mm2(a1, w1, a2, w2, out_dtype):
    m, k1 = a1.shape
    k2 = a2.shape[1]
    n = w1.shape[1]
    tm = _pick(m, (1024, 512, 256, 128))
    tn = _pick(n, (1024, 512, 256, 128))
    return pl.pallas_call(
        _mm2_kernel,
        out_shape=jax.ShapeDtypeStruct((m, n), out_dtype),
        grid=(m // tm, n // tn),
        in_specs=[pl.BlockSpec((tm, k1), lambda i, j: (i, 0)),
                  pl.BlockSpec((k1, tn), lambda i, j: (0, j)),
                  pl.BlockSpec((tm, k2), lambda i, j: (i, 0)),
                  pl.BlockSpec((k2, tn), lambda i, j: (0, j))],
        out_specs=pl.BlockSpec((tm, tn), lambda i, j: (i, j)),
        compiler_params=_params("arbitrary", "arbitrary"),
        name="mm2",
    )(a1, w1, a2, w2)


def _na_bias(rpb, rows):
    w = GRID_W
    h = rpb.shape[0]
    qc = np.arange(w)[:, None]
    kc = np.arange(w)[None, :]
    c0 = np.clip(qc - NA_WIN_C // 2, 0, w - NA_WIN_C)
    ok_c = (kc >= c0) & (kc < c0 + NA_WIN_C)
    dc = np.clip(kc - qc + NA_WIN_C - 1, 0, 2 * NA_WIN_C - 2)
    onehot = (dc[None] == np.arange(2 * NA_WIN_C - 1)[:, None, None]).astype(np.float32)
    t = jnp.einsum('hdj,jqk->hdqk', rpb.astype(F32), onehot, precision=lax.Precision.HIGHEST)
    t = jnp.where(ok_c, t, NEG)
    masked = jnp.full((h, w, w), NEG, F32)
    out = []
    for r_first, k_first in ((0, 0), (NA_Q_ROWS, 0), (rows - NA_Q_ROWS, rows - NA_K_ROWS)):
        q_rows = []
        for qr in range(NA_Q_ROWS):
            r = r_first + qr
            r0 = min(max(r - NA_WIN_R // 2, 0), rows - NA_WIN_R)
            blocks = []
            for kr in range(NA_K_ROWS):
                ka = k_first + kr
                blocks.append(t[:, ka - r + NA_WIN_R - 1] if r0 <= ka < r0 + NA_WIN_R else masked)
            q_rows.append(jnp.concatenate(blocks, axis=2))
        out.append(jnp.concatenate(q_rows, axis=1))
    return jnp.stack(out)


def _na_kernel(q_ref, k0_ref, k1_ref, k2_ref, v0_ref, v1_ref, v2_ref, kc_ref, vc_ref, b_ref, o_ref,
               *, scale, heads):
    qb = q_ref.shape[0]
    hd = [slice(h * NA_HEAD_DIM, (h + 1) * NA_HEAD_DIM) for h in range(heads)]
    qs = [q_ref[:, c] for c in hd]
    s_loc = [jnp.concatenate([_dot_nt(q, k_ref[:, c]) for k_ref in (k0_ref, k1_ref, k2_ref)], axis=1)
             * scale + b_ref[h] for h, (q, c) in enumerate(zip(qs, hd))]
    s_ctx = [_dot_nt(q, kc_ref[:, c]) * scale for q, c in zip(qs, hd)]
    ms = [jnp.maximum(jnp.max(sl, axis=1, keepdims=True), jnp.max(sc, axis=1, keepdims=True))
          for sl, sc in zip(s_loc, s_ctx)]
    p_loc = [jnp.exp(sl - m) for sl, m in zip(s_loc, ms)]
    p_ctx = [jnp.exp(sc - m) for sc, m in zip(s_ctx, ms)]
    for h, c in enumerate(hd):
        denom = jnp.sum(p_loc[h], axis=1, keepdims=True) + jnp.sum(p_ctx[h], axis=1, keepdims=True)
        o = _dot(p_ctx[h].astype(BF16), vc_ref[:, c])
        for j, v_ref in enumerate((v0_ref, v1_ref, v2_ref)):
            o = o + _dot(p_loc[h][:, j * qb:(j + 1) * qb].astype(BF16), v_ref[:, c])
        o_ref[:, c] = (o / denom).astype(o_ref.dtype)


def _na_attention(qkv_l, qkv_c, bias):
    n = qkv_l.shape[0]
    lc = qkv_c.shape[0]
    width = qkv_l.shape[1] // 3
    heads = width // NA_HEAD_DIM
    qb = NA_Q_ROWS * GRID_W
    nblk = n // qb
    assert n % qb == 0 and nblk >= 3
    hb = _pick(heads, (2, 1))
    hg = heads // hb
    bw = hb * NA_HEAD_DIM

    def kv_spec(j, col0):
        return pl.BlockSpec((qb, bw), lambda h, i: (jnp.clip(i - 1, 0, nblk - 3) + j, col0 + h))

    def bias_map(h, i):
        return (jnp.where(i == 0, 0, jnp.where(i == nblk - 1, 2, 1)), h, 0, 0)

    return pl.pallas_call(
        functools.partial(_na_kernel, scale=NA_HEAD_DIM ** -0.5, heads=hb),
        out_shape=jax.ShapeDtypeStruct((n, width), BF16),
        grid=(hg, nblk),
        in_specs=[
            pl.BlockSpec((qb, bw), lambda h, i: (i, h)),
            kv_spec(0, hg), kv_spec(1, hg), kv_spec(2, hg),
            kv_spec(0, 2 * hg), kv_spec(1, 2 * hg), kv_spec(2, 2 * hg),
            pl.BlockSpec((lc, bw), lambda h, i: (0, hg + h)),
            pl.BlockSpec((lc, bw), lambda h, i: (0, 2 * hg + h)),
            pl.BlockSpec((None, hb, qb, 3 * qb), bias_map),
        ],
        out_specs=pl.BlockSpec((qb, bw), lambda h, i: (i, h)),
        compiler_params=_params("arbitrary", "arbitrary"),
        name="na_attention",
    )(qkv_l, qkv_l, qkv_l, qkv_l, qkv_l, qkv_l, qkv_l, qkv_c, qkv_c, bias)


def _dense_attn_kernel(q_ref, k_ref, v_ref, o_ref, *, scale):
    s = _dot_nt(q_ref[...], k_ref[...]) * scale
    p = jnp.exp(s - jnp.max(s, axis=1, keepdims=True))
    o = _dot(p.astype(BF16), v_ref[...])
    o_ref[...] = (o / jnp.sum(p, axis=1, keepdims=True)).astype(o_ref.dtype)


def _dense_attention(qkv):
    n = qkv.shape[0]
    width = qkv.shape[1] // 3
    heads = width // NA_HEAD_DIM
    return pl.pallas_call(
        functools.partial(_dense_attn_kernel, scale=NA_HEAD_DIM ** -0.5),
        out_shape=jax.ShapeDtypeStruct((n, width), BF16),
        grid=(heads,),
        in_specs=[pl.BlockSpec((n, NA_HEAD_DIM), lambda h: (0, h)),
                  pl.BlockSpec((n, NA_HEAD_DIM), lambda h: (0, heads + h)),
                  pl.BlockSpec((n, NA_HEAD_DIM), lambda h: (0, 2 * heads + h))],
        out_specs=pl.BlockSpec((n, NA_HEAD_DIM), lambda h: (0, h)),
        compiler_params=_params("arbitrary"),
        name="dense_attention",
    )(qkv, qkv, qkv)


def _pool_kernel(u_ref, o_ref, *, n, chunk, blocks_per_group):
    group = pl.program_id(0) // blocks_per_group
    rows = chunk + 2 * POOL_HALO

    def window_sum(xs, w):
        a = pltpu.roll(xs, 1, axis=0) + xs
        half = 1
        while 2 * half < w:
            a = pltpu.roll(a, half, axis=0) + pltpu.roll(a, rows - half, axis=0)
            half *= 2
        return a

    for gi, w in enumerate(POOL_WINDOWS):
        @pl.when(group == gi)
        def _(w=w):
            def body(c, carry):
                base = pl.multiple_of(c * chunk, chunk)
                xs = u_ref[pl.ds(base, rows), :]
                tot = window_sum(xs, w)[POOL_HALO:POOL_HALO + chunk]
                t = base + lax.broadcasted_iota(jnp.int32, (chunk, 1), 0)
                cnt = jnp.minimum(t + w // 2, n) - jnp.maximum(t - w // 2, 0)
                mean = tot / cnt.astype(F32)
                o_ref[pl.ds(base, chunk), :] = (mean - xs[POOL_HALO:POOL_HALO + chunk]).astype(o_ref.dtype)
                return carry
            lax.fori_loop(0, n // chunk, body, 0)


def _pool_diff(u):
    n, c = u.shape
    cg = c // len(POOL_WINDOWS)
    assert cg % LANES == 0
    chunk = _pick(n, (512, 256, 128))
    up = jnp.pad(u, ((POOL_HALO, POOL_HALO), (0, 0)))
    return pl.pallas_call(
        functools.partial(_pool_kernel, n=n, chunk=chunk, blocks_per_group=cg // LANES),
        out_shape=jax.ShapeDtypeStruct((n, c), BF16),
        grid=(c // LANES,),
        in_specs=[pl.BlockSpec((n + 2 * POOL_HALO, LANES), lambda j: (0, j))],
        out_specs=pl.BlockSpec((n, LANES), lambda j: (0, j)),
        compiler_params=_params("arbitrary"),
        name="pool_diff",
    )(up)


def _group_mm_kernel(a_ref, w_ref, s_ref, o_ref):
    o_ref[...] = (_dot(a_ref[...], w_ref[...]) * s_ref[...]).astype(o_ref.dtype)


def _group_mm(d, w_grp, ch_scale):
    n, c = d.shape
    g, cg, _ = w_grp.shape
    tm = _pick(n, (1024, 512, 256, 128))
    return pl.pallas_call(
        _group_mm_kernel,
        out_shape=jax.ShapeDtypeStruct((n, c), BF16),
        grid=(g, n // tm),
        in_specs=[pl.BlockSpec((tm, cg), lambda gi, i: (i, gi)),
                  pl.BlockSpec((None, cg, cg), lambda gi, i: (gi, 0, 0)),
                  pl.BlockSpec((1, cg), lambda gi, i: (0, gi))],
        out_specs=pl.BlockSpec((tm, cg), lambda gi, i: (i, gi)),
        compiler_params=_params("arbitrary", "arbitrary"),
        name="group_mm",
    )(d, w_grp, ch_scale)


def _rope_tables(n):
    nf = MLA_ROPE // 4
    t = jnp.arange(n)
    row = (t // GRID_W).astype(F32)
    col = (t % GRID_W).astype(F32)
    freqs = ROPE_THETA ** (-jnp.arange(nf, dtype=F32) / nf)
    ang = jnp.stack([row[:, None] * freqs, col[:, None] * freqs], axis=1)
    cos, sin = jnp.cos(ang), jnp.sin(ang)
    c = jnp.stack([cos, cos], axis=2).reshape(n, MLA_ROPE)
    s = jnp.stack([-sin, sin], axis=2).reshape(n, MLA_ROPE)
    pad = jnp.zeros((n, LANES - MLA_ROPE), F32)
    return jnp.concatenate([c, pad], axis=1), jnp.concatenate([s, pad], axis=1)


def _rope_cols(w):
    nf = MLA_ROPE // 4
    j = jnp.arange(MLA_ROPE)
    partner = (j // (2 * nf)) * (2 * nf) + (1 - (j // nf) % 2) * nf + j % nf
    z = jnp.zeros((w.shape[0], LANES - MLA_ROPE), w.dtype)
    return jnp.concatenate([w, z, w[:, partner], z], axis=1)


def _mla_down_kernel(h_ref, w_ref, gq_ref, gkv_ref, c_ref, s_ref, qn_ref, ckv_ref, kr_ref, *, qr, kvr):
    z = _dot(h_ref[...], w_ref[...])
    qn_ref[...] = (_rms(z[:, :qr]) * gq_ref[...]).astype(qn_ref.dtype)
    ckv_ref[...] = (_rms(z[:, qr:qr + kvr]) * gkv_ref[...]).astype(ckv_ref.dtype)
    r0 = qr + kvr
    rot = z[:, r0:r0 + LANES] * c_ref[...] + z[:, r0 + LANES:r0 + 2 * LANES] * s_ref[...]
    kr_ref[...] = rot.astype(kr_ref.dtype)


def _mla_down(h, w_ext, g_q, g_kv, ctab, stab):
    n, d = h.shape
    qr = g_q.shape[1]
    kvr = g_kv.shape[1]
    nw = w_ext.shape[1]
    tm = _pick(n, (512, 256, 128))
    return pl.pallas_call(
        functools.partial(_mla_down_kernel, qr=qr, kvr=kvr),
        out_shape=[jax.ShapeDtypeStruct((n, qr), BF16),
                   jax.ShapeDtypeStruct((n, kvr), BF16),
                   jax.ShapeDtypeStruct((n, LANES), BF16)],
        grid=(n // tm,),
        in_specs=[pl.BlockSpec((tm, d), lambda i: (i, 0)),
                  pl.BlockSpec((d, nw), lambda i: (0, 0)),
                  pl.BlockSpec((1, qr), lambda i: (0, 0)),
                  pl.BlockSpec((1, kvr), lambda i: (0, 0)),
                  pl.BlockSpec((tm, LANES), lambda i: (i, 0)),
                  pl.BlockSpec((tm, LANES), lambda i: (i, 0))],
        out_specs=[pl.BlockSpec((tm, qr), lambda i: (i, 0)),
                   pl.BlockSpec((tm, kvr), lambda i: (i, 0)),
                   pl.BlockSpec((tm, LANES), lambda i: (i, 0))],
        compiler_params=_params("arbitrary"),
        name="mla_down",
    )(h, w_ext, g_q, g_kv, ctab, stab)


MLA_Q_IN = MLA_NOPE + 2 * LANES
MLA_Q_OUT = MLA_NOPE + LANES


def _q_up_kernel(a_ref, w_ref, c_ref, s_ref, o_ref, *, heads, scale):
    z = _dot(a_ref[...], w_ref[...])
    c = c_ref[...] * scale
    s = s_ref[...] * scale
    for j in range(heads):
        zi = j * MLA_Q_IN
        oi = j * MLA_Q_OUT
        o_ref[:, oi:oi + MLA_NOPE] = (z[:, zi:zi + MLA_NOPE] * scale).astype(o_ref.dtype)
        rot = z[:, zi + MLA_NOPE:zi + MLA_NOPE + LANES] * c + z[:, zi + MLA_NOPE + LANES:zi + MLA_Q_IN] * s
        o_ref[:, oi + MLA_NOPE:oi + MLA_Q_OUT] = rot.astype(o_ref.dtype)


def _q_up(qn, w_ext, ctab, stab, heads):
    n, qr = qn.shape
    tm = _pick(n, (1024, 512, 256, 128))
    hb = _pick(heads, (4, 2, 1))
    scale = (MLA_NOPE + MLA_ROPE) ** -0.5 * LOG2E
    return pl.pallas_call(
        functools.partial(_q_up_kernel, heads=hb, scale=scale),
        out_shape=jax.ShapeDtypeStruct((n, heads * MLA_Q_OUT), BF16),
        grid=(n // tm, heads // hb),
        in_specs=[pl.BlockSpec((tm, qr), lambda i, j: (i, 0)),
                  pl.BlockSpec((qr, hb * MLA_Q_IN), lambda i, j: (0, j)),
                  pl.BlockSpec((tm, LANES), lambda i, j: (i, 0)),
                  pl.BlockSpec((tm, LANES), lambda i, j: (i, 0))],
        out_specs=pl.BlockSpec((tm, hb * MLA_Q_OUT), lambda i, j: (i, j)),
        compiler_params=_params("arbitrary", "arbitrary"),
        name="mla_q_up",
    )(qn, w_ext, ctab, stab)


def _mla_attn_kernel(q_ref, kv_ref, kr_ref, *rest, tk, sub, has_ctx):
    if has_ctx:
        kvc_ref, krc_ref, o_ref, qt_sc, vt_sc, vtc_sc = rest[:6]
    else:
        o_ref, qt_sc, vt_sc = rest[:3]
    slot0, slot1 = rest[-11:-7], rest[-7:-3]
    m_sc, l_sc, acc_sc = rest[-3:]
    tq = q_ref.shape[0]
    nchunks = kv_ref.shape[0] // tk
    cols = [slice(j * sub, (j + 1) * sub) for j in range(tq // sub)]

    @pl.when(pl.program_id(1) == 0)
    def _():
        def tr(c, carry):
            ks = pl.multiple_of(c * tk, tk)
            vt_sc[c] = kv_ref[pl.ds(ks, tk), MLA_NOPE:MLA_NOPE + MLA_V].T
            return carry
        lax.fori_loop(0, nchunks, tr, 0)
        if has_ctx:
            vtc_sc[...] = kvc_ref[:, MLA_NOPE:MLA_NOPE + MLA_V].T

    qt_sc[...] = q_ref[...].T
    m_sc[...] = jnp.full(m_sc.shape, NEG, F32)
    l_sc[...] = jnp.zeros(l_sc.shape, F32)
    acc_sc[...] = jnp.zeros(acc_sc.shape, F32)

    def keys(c):
        ks = pl.multiple_of(c * tk, tk)
        return jnp.concatenate([kv_ref[pl.ds(ks, tk), 0:MLA_NOPE], kr_ref[pl.ds(ks, tk), :]], axis=1)

    def scores(k):
        return [_dot(k, qt_sc[:, c]) for c in cols]

    def softmax_pv(sts, vt):
        m_prev = m_sc[...]
        m_new = jnp.maximum(m_prev, jnp.concatenate(
            [jnp.max(st, axis=0, keepdims=True) for st in sts], axis=1))
        alpha = jnp.exp2(m_prev - m_new)
        sums, pvs = [], []
        for st, c in zip(sts, cols):
            p = jnp.exp2(st - m_new[:, c])
            sums.append(jnp.sum(p, axis=0, keepdims=True))
            pvs.append(_dot(vt, p.astype(BF16)))
        l_sc[...] = alpha * l_sc[...] + jnp.concatenate(sums, axis=1)
        acc_sc[...] = alpha * acc_sc[...] + jnp.concatenate(pvs, axis=1)
        m_sc[...] = m_new

    def qk_stage(c, slot):
        st_sc, mx_sc = slot[0], slot[1]
        for col, st in zip(cols, scores(keys(c))):
            st_sc[:, col] = st
            mx_sc[:, col] = jnp.max(st, axis=0, keepdims=True)

    def softmax_stage(slot):
        st_sc, mx_sc, p_sc, al_sc = slot
        m_prev = m_sc[...]
        m_new = jnp.maximum(m_prev, mx_sc[...])
        alpha = jnp.exp2(m_prev - m_new)
        sums = []
        for col in cols:
            p = jnp.exp2(st_sc[:, col] - m_new[:, col])
            sums.append(jnp.sum(p, axis=0, keepdims=True))
            p_sc[:, col] = p.astype(BF16)
        l_sc[...] = alpha * l_sc[...] + jnp.concatenate(sums, axis=1)
        al_sc[...] = alpha
        m_sc[...] = m_new

    def pv_stage(c, slot):
        p_sc, al_sc = slot[2], slot[3]
        vt = vt_sc[c]
        pv = jnp.concatenate([_dot(vt, p_sc[:, col]) for col in cols], axis=1)
        acc_sc[...] = al_sc[...] * acc_sc[...] + pv

    if has_ctx:
        kc = jnp.concatenate([kvc_ref[:, 0:MLA_NOPE], krc_ref[...]], axis=1)
        softmax_pv(scores(kc), vtc_sc[...])
    if nchunks == 1:
        softmax_pv(scores(keys(0)), vt_sc[0])
    else:
        assert nchunks % 2 == 0
        qk_stage(0, slot0)
        qk_stage(1, slot1)
        softmax_stage(slot0)

        def body(i, carry):
            softmax_stage(slot1)
            qk_stage(2 * i + 2, slot0)
            pv_stage(2 * i, slot0)
            softmax_stage(slot0)
            qk_stage(2 * i + 3, slot1)
            pv_stage(2 * i + 1, slot1)
            return carry

        lax.fori_loop(0, nchunks // 2 - 1, body, 0)
        softmax_stage(slot1)
        pv_stage(nchunks - 2, slot0)
        pv_stage(nchunks - 1, slot1)
    o_ref[...] = (acc_sc[...] / l_sc[...]).T.astype(o_ref.dtype)


def _mla_attention(q, kv, kr, kv_c=None, kr_c=None):
    nq = q.shape[0]
    s = kv.shape[0]
    heads = q.shape[1] // MLA_Q_OUT
    tq = _pick(nq, (1024, 512, 256, 128))
    sub = _pick(tq, (256, 128))
    tk = _pick(s, (512, 256, 128))
    has_ctx = kv_c is not None
    kvw = MLA_NOPE + MLA_V
    in_specs = [pl.BlockSpec((tq, MLA_Q_OUT), lambda h, i: (i, h)),
                pl.BlockSpec((s, kvw), lambda h, i: (0, h)),
                pl.BlockSpec((s, LANES), lambda h, i: (0, 0))]
    args = [q, kv, kr]
    scratch = [pltpu.VMEM((MLA_Q_OUT, tq), BF16), pltpu.VMEM((s // tk, MLA_V, tk), BF16)]
    if has_ctx:
        sc = kv_c.shape[0]
        in_specs += [pl.BlockSpec((sc, kvw), lambda h, i: (0, h)),
                     pl.BlockSpec((sc, LANES), lambda h, i: (0, 0))]
        args += [kv_c, kr_c]
        scratch.append(pltpu.VMEM((MLA_V, sc), BF16))
    slot = [pltpu.VMEM((tk, tq), F32), pltpu.VMEM((1, tq), F32), pltpu.VMEM((tk, tq), BF16), pltpu.VMEM((1, tq), F32)]
    scratch += slot + slot
    scratch += [pltpu.VMEM((1, tq), F32), pltpu.VMEM((1, tq), F32), pltpu.VMEM((MLA_V, tq), F32)]
    return pl.pallas_call(
        functools.partial(_mla_attn_kernel, tk=tk, sub=sub, has_ctx=has_ctx),
        out_shape=jax.ShapeDtypeStruct((nq, heads * MLA_V), BF16),
        grid=(heads, nq // tq),
        in_specs=in_specs,
        out_specs=pl.BlockSpec((tq, MLA_V), lambda h, i: (i, h)),
        scratch_shapes=scratch,
        compiler_params=_params("arbitrary", "arbitrary"),
        name="mla_attention",
    )(*args)


GLU_HALO = 16


def _glu_down_kernel(g_ref, v_ref, gp_ref, gn_ref, cw_ref, cb_ref, w_ref, o_ref):
    i = pl.program_id(0)
    k = pl.program_id(1)
    tm = g_ref.shape[0]

    @pl.when(k == 0)
    def _():
        o_ref[...] = jnp.zeros(o_ref.shape, F32)

    g = g_ref[...].astype(F32)
    prev_row = jnp.where(i > 0, gp_ref[GLU_HALO - 1:GLU_HALO, :].astype(F32), 0.0)
    next_row = jnp.where(i < pl.num_programs(0) - 1, gn_ref[0:1, :].astype(F32), 0.0)
    row = lax.broadcasted_iota(jnp.int32, (tm, 1), 0)
    g_dn = jnp.where(row == 0, prev_row, pltpu.roll(g, 1, axis=0))
    g_up = jnp.where(row == tm - 1, next_row, pltpu.roll(g, tm - 1, axis=0))
    cw = cw_ref[...]
    gc = g_dn * cw[0:1, :] + g * cw[1:2, :] + g_up * cw[2:3, :] + cb_ref[...]
    a = (gc * jax.nn.sigmoid(gc)) * v_ref[...].astype(F32)
    o_ref[...] = _dot(a.astype(BF16), w_ref[...]) + o_ref[...]


def _glu_down(u, conv_w, conv_b, w_down):
    n = u.shape[0]
    dff, d = w_down.shape
    tm = _pick(n, (512, 256, 128))
    tk = _pick(dff, (1024, 512, 256, 128))
    nk = dff // tk
    hb = tm // GLU_HALO
    nhb = n // GLU_HALO
    return pl.pallas_call(
        _glu_down_kernel,
        out_shape=jax.ShapeDtypeStruct((n, d), F32),
        grid=(n // tm, nk),
        in_specs=[pl.BlockSpec((tm, tk), lambda i, k: (i, k)),
                  pl.BlockSpec((tm, tk), lambda i, k: (i, nk + k)),
                  pl.BlockSpec((GLU_HALO, tk), lambda i, k: (jnp.maximum(i * hb - 1, 0), k)),
                  pl.BlockSpec((GLU_HALO, tk), lambda i, k: (jnp.minimum((i + 1) * hb, nhb - 1), k)),
                  pl.BlockSpec((3, tk), lambda i, k: (0, k)),
                  pl.BlockSpec((1, tk), lambda i, k: (0, k)),
                  pl.BlockSpec((tk, d), lambda i, k: (k, 0))],
        out_specs=pl.BlockSpec((tm, d), lambda i, k: (i, 0)),
        compiler_params=_params("arbitrary", "arbitrary"),
        name="glu_down",
    )(u, u, u, u, conv_w, conv_b, w_down)


def _conv_glu(h, w_up, conv_w, conv_b, w_down):
    return _glu_down(_mm(h, w_up, BF16), conv_w, conv_b, w_down)


def _ab_mixer(hl, hc, w_in, rpb, pool_w, pool_scale, w_out, need_ctx):
    n = hl.shape[0]
    naw = w_out.shape[0] // 2
    w_qkv = w_in[:, :3 * naw].astype(BF16)
    w_u = w_in[:, 3 * naw:].astype(BF16)
    w_out_a = w_out[:naw].astype(BF16)
    w_out_b = w_out[naw:].astype(BF16)
    pool_w = pool_w.astype(BF16)
    pool_scale = pool_scale.reshape(1, -1)
    qkv_l = _mm(hl, w_qkv, BF16)
    qkv_c = _mm(hc, w_qkv, BF16)
    o_a = _na_attention(qkv_l, qkv_c, _na_bias(rpb, n // GRID_W))
    o_b = _group_mm(_pool_diff(_mm(hl, w_u, F32)), pool_w, pool_scale)
    yl = _mm2(o_a, w_out_a, o_b, w_out_b, F32)
    yc = None
    if need_ctx:
        o_ac = _dense_attention(qkv_c)
        o_bc = _group_mm(_pool_diff(_mm(hc, w_u, F32)), pool_w, pool_scale)
        yc = _mm2(o_ac, w_out_a, o_bc, w_out_b, F32)
    return yl, yc


def _mla_mixer(hl, hc, w_down, g_q, g_kv, w_uq, w_ukv, w_out, tabs_l, tabs_c, need_ctx):
    qr = g_q.shape[0]
    kvr = g_kv.shape[0]
    heads = w_ukv.shape[1] // (MLA_NOPE + MLA_V)
    w_down_ext = jnp.concatenate([w_down[:, :qr + kvr], _rope_cols(w_down[:, qr + kvr:])], axis=1).astype(BF16)
    w_uq_h = w_uq.reshape(qr, heads, MLA_NOPE + MLA_ROPE)
    w_uq_ext = jnp.concatenate(
        [w_uq_h[..., :MLA_NOPE],
         _rope_cols(w_uq_h[..., MLA_NOPE:].reshape(qr * heads, MLA_ROPE)).reshape(qr, heads, 2 * LANES)],
        axis=-1).reshape(qr, heads * MLA_Q_IN).astype(BF16)
    w_ukv = w_ukv.astype(BF16)
    w_out = w_out.astype(BF16)
    g_q = g_q.reshape(1, -1)
    g_kv = g_kv.reshape(1, -1)
    qn_l, ckv_l, kr_l = _mla_down(hl, w_down_ext, g_q, g_kv, *tabs_l)
    qn_c, ckv_c, kr_c = _mla_down(hc, w_down_ext, g_q, g_kv, *tabs_c)
    q_l = _q_up(qn_l, w_uq_ext, *tabs_l, heads)
    kv_l = _mm(ckv_l, w_ukv, BF16)
    kv_c = _mm(ckv_c, w_ukv, BF16)
    yl = _mm(_mla_attention(q_l, kv_l, kr_l, kv_c, kr_c), w_out, F32)
    yc = None
    if need_ctx:
        q_c = _q_up(qn_c, w_uq_ext, *tabs_c, heads)
        yc = _mm(_mla_attention(q_c, kv_c, kr_c), w_out, F32)
    return yl, yc


def kernel(x, c, ctx, c_ctx, w_ada, b_ada, g_mix_pre, g_mix_post, g_ffn_pre, g_ffn_post, ab_w_in, na_rpb, pool_w, pool_scale, ab_w_out, mla_w_down, mla_g_q, mla_g_kv, mla_w_uq, mla_w_ukv, mla_w_out, ffn_w_up, ffn_conv_w, ffn_conv_b, ffn_w_down):
    batch, n, d = x.shape
    lc = ctx.shape[1]
    depth = w_ada.shape[0]
    assert batch == 1 and c.shape[0] == 1
    xl = x.reshape(n, d)
    xc = ctx.reshape(lc, d)

    cvec = jnp.concatenate([c, c_ctx[None, :], jnp.zeros((6, d), F32)], axis=0)
    mod = _ada(cvec, w_ada, b_ada)

    def mods(l, r):
        return [mod[l, r:r + 1, j * d:(j + 1) * d] for j in range(6)]

    tabs_l = _rope_tables(n)
    tabs_c = (jnp.concatenate([jnp.ones((lc, MLA_ROPE), F32), jnp.zeros((lc, LANES - MLA_ROPE), F32)], axis=1),
              jnp.zeros((lc, LANES), F32))

    vec = lambda a: a.reshape(1, -1)
    sh1, sc1, _, _, _, _ = mods(0, 0)
    sh1c, sc1c, _, _, _, _ = mods(0, 1)
    hl = _norm_mod(xl, vec(g_mix_pre[0]), sc1, sh1)
    hc = _norm_mod(xc, vec(g_mix_pre[0]), sc1c, sh1c)
    for l in range(depth):
        last = l == depth - 1
        sh1, sc1, gt1, sh2, sc2, gt2 = mods(l, 0)
        sh1c, sc1c, gt1c, sh2c, sc2c, gt2c = mods(l, 1)
        if l % 2 == 0:
            e = l // 2
            yl, yc = _ab_mixer(hl, hc, ab_w_in[e], na_rpb[e], pool_w[e], pool_scale[e], ab_w_out[e], not last)
        else:
            o = l // 2
            yl, yc = _mla_mixer(hl, hc, mla_w_down[o], mla_g_q[o], mla_g_kv[o], mla_w_uq[o], mla_w_ukv[o],
                                mla_w_out[o], tabs_l, tabs_c, not last)
        w_up = ffn_w_up[l].astype(BF16)
        w_dn = ffn_w_down[l].astype(BF16)
        conv_b = vec(ffn_conv_b[l])
        xl, h2 = _resid(xl, yl, vec(g_mix_post[l]), gt1, (vec(g_ffn_pre[l]), sc2, sh2))
        f = _conv_glu(h2, w_up, ffn_conv_w[l], conv_b, w_dn)
        if last:
            xl, _ = _resid(xl, f, vec(g_ffn_post[l]), gt2)
        else:
            nsh1, nsc1, _, _, _, _ = mods(l + 1, 0)
            xl, hl = _resid(xl, f, vec(g_ffn_post[l]), gt2, (vec(g_mix_pre[l + 1]), nsc1, nsh1))
            xc, h2c = _resid(xc, yc, vec(g_mix_post[l]), gt1c, (vec(g_ffn_pre[l]), sc2c, sh2c))
            fc = _conv_glu(h2c, w_up, ffn_conv_w[l], conv_b, w_dn)
            nsh1c, nsc1c, _, _, _, _ = mods(l + 1, 1)
            xc, hc = _resid(xc, fc, vec(g_ffn_post[l]), gt2c, (vec(g_mix_pre[l + 1]), nsc1c, nsh1c))
    return xl.reshape(batch, n, d)
```

```python
import functools

import numpy as np
import jax
import jax.numpy as jnp
from jax import lax
from jax.experimental import pallas as pl
from jax.experimental.pallas import tpu as pltpu

GRID_W = 64
EPS = 1e-6
NA_HEAD_DIM = 128
NA_WIN_R = 8
NA_WIN_C = 16
NA_Q_ROWS = 4
NA_K_ROWS = NA_Q_ROWS + NA_WIN_R
POOL_WINDOWS = (2, 4, 8, 16)
POOL_HALO = 8
MLA_NOPE = 128
MLA_ROPE = 64
MLA_V = 128
ROPE_THETA = 10000.0
LANES = 128
NEG = -1e30
LOG2E = 1.4426950408889634
VMEM_LIMIT = 56 * 1024 * 1024

F32 = jnp.float32
BF16 = jnp.bfloat16


def _pick(n, prefs):
    for p in prefs:
        if n % p == 0:
            return p
    return n


def _params(*sem):
    return pltpu.CompilerParams(dimension_semantics=sem, vmem_limit_bytes=VMEM_LIMIT)


def _rms(x):
    return x * lax.rsqrt(jnp.mean(x * x, axis=-1, keepdims=True) + EPS)


def _dot(a, b):
    return jnp.dot(a, b, preferred_element_type=F32)


def _dot_nt(a, b):
    return lax.dot_general(a, b, (((1,), (1,)), ((), ())), preferred_element_type=F32)


def _ada_kernel(c_ref, w_ref, b_ref, o_ref):
    c = c_ref[...]
    s = (c * jax.nn.sigmoid(c)).astype(BF16)
    o_ref[...] = _dot(s, w_ref[...].astype(BF16)) + b_ref[...]


def _ada(cvec, w_ada, b_ada):
    depth, d, n6 = w_ada.shape
    rows = cvec.shape[0]
    tn = _pick(n6, (512, 256, 128))
    return pl.pallas_call(
        _ada_kernel,
        out_shape=jax.ShapeDtypeStruct((depth, rows, n6), F32),
        grid=(depth, n6 // tn),
        in_specs=[
            pl.BlockSpec((rows, d), lambda l, j: (0, 0)),
            pl.BlockSpec((None, d, tn), lambda l, j: (l, 0, j)),
            pl.BlockSpec((None, 1, tn), lambda l, j: (l, 0, j)),
        ],
        out_specs=pl.BlockSpec((None, rows, tn), lambda l, j: (l, 0, j)),
        compiler_params=_params("arbitrary", "arbitrary"),
        name="ada",
    )(cvec, w_ada, b_ada.reshape(depth, 1, n6))


def _norm_mod_kernel(x_ref, g_ref, sc_ref, sh_ref, o_ref):
    y = _rms(x_ref[...]) * g_ref[...]
    o_ref[...] = (y * (1.0 + sc_ref[...]) + sh_ref[...]).astype(o_ref.dtype)


def _norm_mod(x, g, sc, sh):
    m, d = x.shape
    tm = _pick(m, (256, 128, 64, 32, 16))
    vec = pl.BlockSpec((1, d), lambda i: (0, 0))
    return pl.pallas_call(
        _norm_mod_kernel,
        out_shape=jax.ShapeDtypeStruct((m, d), BF16),
        grid=(m // tm,),
        in_specs=[pl.BlockSpec((tm, d), lambda i: (i, 0)), vec, vec, vec],
        out_specs=pl.BlockSpec((tm, d), lambda i: (i, 0)),
        compiler_params=_params("arbitrary"),
        name="norm_mod",
    )(x, g, sc, sh)


def _resid_kernel(x_ref, y_ref, gpost_ref, gt_ref, *rest, with_h):
    xn = x_ref[...] + gt_ref[...] * (_rms(y_ref[...]) * gpost_ref[...])
    if with_h:
        gpre_ref, sc_ref, sh_ref, xo_ref, ho_ref = rest
        xo_ref[...] = xn
        h = _rms(xn) * gpre_ref[...]
        ho_ref[...] = (h * (1.0 + sc_ref[...]) + sh_ref[...]).astype(ho_ref.dtype)
    else:
        (xo_ref,) = rest
        xo_ref[...] = xn


def _resid(x, y, gpost, gt, pre=None):
    m, d = x.shape
    tm = _pick(m, (256, 128, 64, 32, 16))
    vec = pl.BlockSpec((1, d), lambda i: (0, 0))
    row = pl.BlockSpec((tm, d), lambda i: (i, 0))
    with_h = pre is not None
    out_shape = [jax.ShapeDtypeStruct((m, d), F32)]
    out_specs = [row]
    args = [x, y, gpost, gt]
    in_specs = [row, row, vec, vec]
    if with_h:
        out_shape.append(jax.ShapeDtypeStruct((m, d), BF16))
        out_specs.append(row)
        args += list(pre)
        in_specs += [vec, vec, vec]
    out = pl.pallas_call(
        functools.partial(_resid_kernel, with_h=with_h),
        out_shape=out_shape,
        grid=(m // tm,),
        in_specs=in_specs,
        out_specs=out_specs,
        compiler_params=_params("arbitrary"),
        name="resid",
    )(*args)
    return (out[0], out[1]) if with_h else (out[0], None)


def _mm_kernel(a_ref, w_ref, o_ref):
    o_ref[...] = _dot(a_ref[...], w_ref[...]).astype(o_ref.dtype)


def _mm(a, w, out_dtype):
    m, k = a.shape
    n = w.shape[1]
    tm = _pick(m, (1024, 1280, 512, 256, 128))
    tn = _pick(n, (1024, 512, 256, 128))
    return pl.pallas_call(
        _mm_kernel,
        out_shape=jax.ShapeDtypeStruct((m, n), out_dtype),
        grid=(m // tm, n // tn),
        in_specs=[pl.BlockSpec((tm, k), lambda i, j: (i, 0)),
                  pl.BlockSpec((k, tn), lambda i, j: (0, j))],
        out_specs=pl.BlockSpec((tm, tn), lambda i, j: (i, j)),
        compiler_params=_params("arbitrary", "arbitrary"),
        name="mm",
    )(a, w)


def _mm2_kernel(a1_ref, w1_ref, a2_ref, w2_ref, o_ref):
    o_ref[...] = (_dot(a1_ref[...], w1_ref[...]) + _dot(a2_ref[...], w2_ref[...])).astype(o_ref.dtype)


def _mm2(a1, w1, a2, w2, out_dtype):
    m, k1 = a1.shape
    k2 = a2.shape[1]
    n = w1.shape[1]
    tm = _pick(m, (1024, 512, 256, 128))
    tn = _pick(n, (1024, 512, 256, 128))
    return pl.pallas_call(
        _mm2_kernel,
        out_shape=jax.ShapeDtypeStruct((m, n), out_dtype),
        grid=(m // tm, n // tn),
        in_specs=[pl.BlockSpec((tm, k1), lambda i, j: (i, 0)),
                  pl.BlockSpec((k1, tn), lambda i, j: (0, j)),
                  pl.BlockSpec((tm, k2), lambda i, j: (i, 0)),
                  pl.BlockSpec((k2, tn), lambda i, j: (0, j))],
        out_specs=pl.BlockSpec((tm, tn), lambda i, j: (i, j)),
        compiler_params=_params("arbitrary", "arbitrary"),
        name="mm2",
    )(a1, w1, a2, w2)


def _na_bias(rpb, rows):
    w = GRID_W
    h = rpb.shape[0]
    qc = np.arange(w)[:, None]
    kc = np.arange(w)[None, :]
    c0 = np.clip(qc - NA_WIN_C // 2, 0, w - NA_WIN_C)
    ok_c = (kc >= c0) & (kc < c0 + NA_WIN_C)
    dc = np.clip(kc - qc + NA_WIN_C - 1, 0, 2 * NA_WIN_C - 2)
    onehot = (dc[None] == np.arange(2 * NA_WIN_C - 1)[:, None, None]).astype(np.float32)
    t = jnp.einsum('hdj,jqk->hdqk', rpb.astype(F32), onehot, precision=lax.Precision.HIGHEST)
    t = jnp.where(ok_c, t, NEG)
    masked = jnp.full((h, w, w), NEG, F32)
    out = []
    for r_first, k_first in ((0, 0), (NA_Q_ROWS, 0), (rows - NA_Q_ROWS, rows - NA_K_ROWS)):
        q_rows = []
        for qr in range(NA_Q_ROWS):
            r = r_first + qr
            r0 = min(max(r - NA_WIN_R // 2, 0), rows - NA_WIN_R)
            blocks = []
            for kr in range(NA_K_ROWS):
                ka = k_first + kr
                blocks.append(t[:, ka - r + NA_WIN_R - 1] if r0 <= ka < r0 + NA_WIN_R else masked)
            q_rows.append(jnp.concatenate(blocks, axis=2))
        out.append(jnp.concatenate(q_rows, axis=1))
    return jnp.stack(out)


def _na_kernel(q_ref, k0_ref, k1_ref, k2_ref, v0_ref, v1_ref, v2_ref, kc_ref, vc_ref, b_ref, o_ref,
               *, scale, heads):
    qb = q_ref.shape[0]
    hd = [slice(h * NA_HEAD_DIM, (h + 1) * NA_HEAD_DIM) for h in range(heads)]
    qs = [q_ref[:, c] for c in hd]
    s_loc = [jnp.concatenate([_dot_nt(q, k_ref[:, c]) for k_ref in (k0_ref, k1_ref, k2_ref)], axis=1)
             * scale + b_ref[h] for h, (q, c) in enumerate(zip(qs, hd))]
    s_ctx = [_dot_nt(q, kc_ref[:, c]) * scale for q, c in zip(qs, hd)]
    ms = [jnp.maximum(jnp.max(sl, axis=1, keepdims=True), jnp.max(sc, axis=1, keepdims=True))
          for sl, sc in zip(s_loc, s_ctx)]
    p_loc = [jnp.exp(sl - m) for sl, m in zip(s_loc, ms)]
    p_ctx = [jnp.exp(sc - m) for sc, m in zip(s_ctx, ms)]
    for h, c in enumerate(hd):
        denom = jnp.sum(p_loc[h], axis=1, keepdims=True) + jnp.sum(p_ctx[h], axis=1, keepdims=True)
        o = _dot(p_ctx[h].astype(BF16), vc_ref[:, c])
        for j, v_ref in enumerate((v0_ref, v1_ref, v2_ref)):
            o = o + _dot(p_loc[h][:, j * qb:(j + 1) * qb].astype(BF16), v_ref[:, c])
        o_ref[:, c] = (o / denom).astype(o_ref.dtype)


def _na_attention(qkv_l, qkv_c, bias):
    n = qkv_l.shape[0]
    lc = qkv_c.shape[0]
    width = qkv_l.shape[1] // 3
    heads = width // NA_HEAD_DIM
    qb = NA_Q_ROWS * GRID_W
    nblk = n // qb
    assert n % qb == 0 and nblk >= 3
    hb = _pick(heads, (2, 1))
    hg = heads // hb
    bw = hb * NA_HEAD_DIM

    def kv_spec(j, col0):
        return pl.BlockSpec((qb, bw), lambda h, i: (jnp.clip(i - 1, 0, nblk - 3) + j, col0 + h))

    def bias_map(h, i):
        return (jnp.where(i == 0, 0, jnp.where(i == nblk - 1, 2, 1)), h, 0, 0)

    return pl.pallas_call(
        functools.partial(_na_kernel, scale=NA_HEAD_DIM ** -0.5, heads=hb),
        out_shape=jax.ShapeDtypeStruct((n, width), BF16),
        grid=(hg, nblk),
        in_specs=[
            pl.BlockSpec((qb, bw), lambda h, i: (i, h)),
            kv_spec(0, hg), kv_spec(1, hg), kv_spec(2, hg),
            kv_spec(0, 2 * hg), kv_spec(1, 2 * hg), kv_spec(2, 2 * hg),
            pl.BlockSpec((lc, bw), lambda h, i: (0, hg + h)),
            pl.BlockSpec((lc, bw), lambda h, i: (0, 2 * hg + h)),
            pl.BlockSpec((None, hb, qb, 3 * qb), bias_map),
        ],
        out_specs=pl.BlockSpec((qb, bw), lambda h, i: (i, h)),
        compiler_params=_params("arbitrary", "arbitrary"),
        name="na_attention",
    )(qkv_l, qkv_l, qkv_l, qkv_l, qkv_l, qkv_l, qkv_l, qkv_c, qkv_c, bias)


def _dense_attn_kernel(q_ref, k_ref, v_ref, o_ref, *, scale):
    s = _dot_nt(q_ref[...], k_ref[...]) * scale
    p = jnp.exp(s - jnp.max(s, axis=1, keepdims=True))
    o = _dot(p.astype(BF16), v_ref[...])
    o_ref[...] = (o / jnp.sum(p, axis=1, keepdims=True)).astype(o_ref.dtype)


def _dense_attention(qkv):
    n = qkv.shape[0]
    width = qkv.shape[1] // 3
    heads = width // NA_HEAD_DIM
    return pl.pallas_call(
        functools.partial(_dense_attn_kernel, scale=NA_HEAD_DIM ** -0.5),
        out_shape=jax.ShapeDtypeStruct((n, width), BF16),
        grid=(heads,),
        in_specs=[pl.BlockSpec((n, NA_HEAD_DIM), lambda h: (0, h)),
                  pl.BlockSpec((n, NA_HEAD_DIM), lambda h: (0, heads + h)),
                  pl.BlockSpec((n, NA_HEAD_DIM), lambda h: (0, 2 * heads + h))],
        out_specs=pl.BlockSpec((n, NA_HEAD_DIM), lambda h: (0, h)),
        compiler_params=_params("arbitrary"),
        name="dense_attention",
    )(qkv, qkv, qkv)


def _pool_kernel(u_ref, o_ref, *, n, chunk, blocks_per_group):
    group = pl.program_id(0) // blocks_per_group
    rows = chunk + 2 * POOL_HALO

    def window_sum(xs, w):
        a = pltpu.roll(xs, 1, axis=0) + xs
        half = 1
        while 2 * half < w:
            a = pltpu.roll(a, half, axis=0) + pltpu.roll(a, rows - half, axis=0)
            half *= 2
        return a

    for gi, w in enumerate(POOL_WINDOWS):
        @pl.when(group == gi)
        def _(w=w):
            def body(c, carry):
                base = pl.multiple_of(c * chunk, chunk)
                xs = u_ref[pl.ds(base, rows), :]
                tot = window_sum(xs, w)[POOL_HALO:POOL_HALO + chunk]
                t = base + lax.broadcasted_iota(jnp.int32, (chunk, 1), 0)
                cnt = jnp.minimum(t + w // 2, n) - jnp.maximum(t - w // 2, 0)
                mean = tot / cnt.astype(F32)
                o_ref[pl.ds(base, chunk), :] = (mean - xs[POOL_HALO:POOL_HALO + chunk]).astype(o_ref.dtype)
                return carry
            lax.fori_loop(0, n // chunk, body, 0)


def _pool_diff(u):
    n, c = u.shape
    cg = c // len(POOL_WINDOWS)
    assert cg % LANES == 0
    chunk = _pick(n, (512, 256, 128))
    up = jnp.pad(u, ((POOL_HALO, POOL_HALO), (0, 0)))
    return pl.pallas_call(
        functools.partial(_pool_kernel, n=n, chunk=chunk, blocks_per_group=cg // LANES),
        out_shape=jax.ShapeDtypeStruct((n, c), BF16),
        grid=(c // LANES,),
        in_specs=[pl.BlockSpec((n + 2 * POOL_HALO, LANES), lambda j: (0, j))],
        out_specs=pl.BlockSpec((n, LANES), lambda j: (0, j)),
        compiler_params=_params("arbitrary"),
        name="pool_diff",
    )(up)


def _group_mm_kernel(a_ref, w_ref, s_ref, o_ref):
    o_ref[...] = (_dot(a_ref[...], w_ref[...]) * s_ref[...]).astype(o_ref.dtype)


def _group_mm(d, w_grp, ch_scale):
    n, c = d.shape
    g, cg, _ = w_grp.shape
    tm = _pick(n, (1024, 512, 256, 128))
    return pl.pallas_call(
        _group_mm_kernel,
        out_shape=jax.ShapeDtypeStruct((n, c), BF16),
        grid=(g, n // tm),
        in_specs=[pl.BlockSpec((tm, cg), lambda gi, i: (i, gi)),
                  pl.BlockSpec((None, cg, cg), lambda gi, i: (gi, 0, 0)),
                  pl.BlockSpec((1, cg), lambda gi, i: (0, gi))],
        out_specs=pl.BlockSpec((tm, cg), lambda gi, i: (i, gi)),
        compiler_params=_params("arbitrary", "arbitrary"),
        name="group_mm",
    )(d, w_grp, ch_scale)


def _rope_tables(n):
    nf = MLA_ROPE // 4
    t = jnp.arange(n)
    row = (t // GRID_W).astype(F32)
    col = (t % GRID_W).astype(F32)
    freqs = ROPE_THETA ** (-jnp.arange(nf, dtype=F32) / nf)
    ang = jnp.stack([row[:, None] * freqs, col[:, None] * freqs], axis=1)
    cos, sin = jnp.cos(ang), jnp.sin(ang)
    c = jnp.stack([cos, cos], axis=2).reshape(n, MLA_ROPE)
    s = jnp.stack([-sin, sin], axis=2).reshape(n, MLA_ROPE)
    pad = jnp.zeros((n, LANES - MLA_ROPE), F32)
    return jnp.concatenate([c, pad], axis=1), jnp.concatenate([s, pad], axis=1)


def _rope_cols(w):
    nf = MLA_ROPE // 4
    j = jnp.arange(MLA_ROPE)
    partner = (j // (2 * nf)) * (2 * nf) + (1 - (j // nf) % 2) * nf + j % nf
    z = jnp.zeros((w.shape[0], LANES - MLA_ROPE), w.dtype)
    return jnp.concatenate([w, z, w[:, partner], z], axis=1)


def _mla_down_kernel(h_ref, w_ref, gq_ref, gkv_ref, c_ref, s_ref, qn_ref, ckv_ref, kr_ref, *, qr, kvr):
    z = _dot(h_ref[...], w_ref[...])
    qn_ref[...] = (_rms(z[:, :qr]) * gq_ref[...]).astype(qn_ref.dtype)
    ckv_ref[...] = (_rms(z[:, qr:qr + kvr]) * gkv_ref[...]).astype(ckv_ref.dtype)
    r0 = qr + kvr
    rot = z[:, r0:r0 + LANES] * c_ref[...] + z[:, r0 + LANES:r0 + 2 * LANES] * s_ref[...]
    kr_ref[...] = rot.astype(kr_ref.dtype)


def _mla_down(h, w_ext, g_q, g_kv, ctab, stab):
    n, d = h.shape
    qr = g_q.shape[1]
    kvr = g_kv.shape[1]
    nw = w_ext.shape[1]
    tm = _pick(n, (512, 256, 128))
    return pl.pallas_call(
        functools.partial(_mla_down_kernel, qr=qr, kvr=kvr),
        out_shape=[jax.ShapeDtypeStruct((n, qr), BF16),
                   jax.ShapeDtypeStruct((n, kvr), BF16),
                   jax.ShapeDtypeStruct((n, LANES), BF16)],
        grid=(n // tm,),
        in_specs=[pl.BlockSpec((tm, d), lambda i: (i, 0)),
                  pl.BlockSpec((d, nw), lambda i: (0, 0)),
                  pl.BlockSpec((1, qr), lambda i: (0, 0)),
                  pl.BlockSpec((1, kvr), lambda i: (0, 0)),
                  pl.BlockSpec((tm, LANES), lambda i: (i, 0)),
                  pl.BlockSpec((tm, LANES), lambda i: (i, 0))],
        out_specs=[pl.BlockSpec((tm, qr), lambda i: (i, 0)),
                   pl.BlockSpec((tm, kvr), lambda i: (i, 0)),
                   pl.BlockSpec((tm, LANES), lambda i: (i, 0))],
        compiler_params=_params("arbitrary"),
        name="mla_down",
    )(h, w_ext, g_q, g_kv, ctab, stab)


MLA_Q_IN = MLA_NOPE + 2 * LANES
MLA_Q_OUT = MLA_NOPE + LANES


def _q_up_kernel(a_ref, w_ref, c_ref, s_ref, o_ref, *, heads, scale):
    z = _dot(a_ref[...], w_ref[...])
    c = c_ref[...] * scale
    s = s_ref[...] * scale
    for j in range(heads):
        zi = j * MLA_Q_IN
        oi = j * MLA_Q_OUT
        o_ref[:, oi:oi + MLA_NOPE] = (z[:, zi:zi + MLA_NOPE] * scale).astype(o_ref.dtype)
        rot = z[:, zi + MLA_NOPE:zi + MLA_NOPE + LANES] * c + z[:, zi + MLA_NOPE + LANES:zi + MLA_Q_IN] * s
        o_ref[:, oi + MLA_NOPE:oi + MLA_Q_OUT] = rot.astype(o_ref.dtype)


def _q_up(qn, w_ext, ctab, stab, heads):
    n, qr = qn.shape
    tm = _pick(n, (1024, 512, 256, 128))
    hb = _pick(heads, (4, 2, 1))
    scale = (MLA_NOPE + MLA_ROPE) ** -0.5 * LOG2E
    return pl.pallas_call(
        functools.partial(_q_up_kernel, heads=hb, scale=scale),
        out_shape=jax.ShapeDtypeStruct((n, heads * MLA_Q_OUT), BF16),
        grid=(n // tm, heads // hb),
        in_specs=[pl.BlockSpec((tm, qr), lambda i, j: (i, 0)),
                  pl.BlockSpec((qr, hb * MLA_Q_IN), lambda i, j: (0, j)),
                  pl.BlockSpec((tm, LANES), lambda i, j: (i, 0)),
                  pl.BlockSpec((tm, LANES), lambda i, j: (i, 0))],
        out_specs=pl.BlockSpec((tm, hb * MLA_Q_OUT), lambda i, j: (i, j)),
        compiler_params=_params("arbitrary", "arbitrary"),
        name="mla_q_up",
    )(qn, w_ext, ctab, stab)


def _mla_attn_kernel(q_ref, kv_ref, kr_ref, *rest, tk, sub, has_ctx):
    if has_ctx:
        kvc_ref, krc_ref, o_ref, qt_sc, vt_sc, vtc_sc = rest[:6]
    else:
        o_ref, qt_sc, vt_sc = rest[:3]
    slot0, slot1 = rest[-11:-7], rest[-7:-3]
    m_sc, l_sc, acc_sc = rest[-3:]
    tq = q_ref.shape[0]
    nchunks = kv_ref.shape[0] // tk
    cols = [slice(j * sub, (j + 1) * sub) for j in range(tq // sub)]

    @pl.when(pl.program_id(1) == 0)
    def _():
        def tr(c, carry):
            ks = pl.multiple_of(c * tk, tk)
            vt_sc[c] = kv_ref[pl.ds(ks, tk), MLA_NOPE:MLA_NOPE + MLA_V].T
            return carry
        lax.fori_loop(0, nchunks, tr, 0)
        if has_ctx:
            vtc_sc[...] = kvc_ref[:, MLA_NOPE:MLA_NOPE + MLA_V].T

    qt_sc[...] = q_ref[...].T
    m_sc[...] = jnp.full(m_sc.shape, NEG, F32)
    l_sc[...] = jnp.zeros(l_sc.shape, F32)
    acc_sc[...] = jnp.zeros(acc_sc.shape, F32)

    def keys(c):
        ks = pl.multiple_of(c * tk, tk)
        return jnp.concatenate([kv_ref[pl.ds(ks, tk), 0:MLA_NOPE], kr_ref[pl.ds(ks, tk), :]], axis=1)

    def scores(k):
        return [_dot(k, qt_sc[:, c]) for c in cols]

    def softmax_pv(sts, vt):
        m_prev = m_sc[...]
        m_new = jnp.maximum(m_prev, jnp.concatenate(
            [jnp.max(st, axis=0, keepdims=True) for st in sts], axis=1))
        alpha = jnp.exp2(m_prev - m_new)
        sums, pvs = [], []
        for st, c in zip(sts, cols):
            p = jnp.exp2(st - m_new[:, c])
            sums.append(jnp.sum(p, axis=0, keepdims=True))
            pvs.append(_dot(vt, p.astype(BF16)))
        l_sc[...] = alpha * l_sc[...] + jnp.concatenate(sums, axis=1)
        acc_sc[...] = alpha * acc_sc[...] + jnp.concatenate(pvs, axis=1)
        m_sc[...] = m_new

    def qk_stage(c, slot):
        st_sc, mx_sc = slot[0], slot[1]
        for col, st in zip(cols, scores(keys(c))):
            st_sc[:, col] = st
            mx_sc[:, col] = jnp.max(st, axis=0, keepdims=True)

    def softmax_stage(slot):
        st_sc, mx_sc, p_sc, al_sc = slot
        m_prev = m_sc[...]
        m_new = jnp.maximum(m_prev, mx_sc[...])
        alpha = jnp.exp2(m_prev - m_new)
        sums = []
        for col in cols:
            p = jnp.exp2(st_sc[:, col] - m_new[:, col])
            sums.append(jnp.sum(p, axis=0, keepdims=True))
            p_sc[:, col] = p.astype(BF16)
        l_sc[...] = alpha * l_sc[...] + jnp.concatenate(sums, axis=1)
        al_sc[...] = alpha
        m_sc[...] = m_new

    def pv_stage(c, slot):
        p_sc, al_sc = slot[2], slot[3]
        vt = vt_sc[c]
        pv = jnp.concatenate([_dot(vt, p_sc[:, col]) for col in cols], axis=1)
        acc_sc[...] = al_sc[...] * acc_sc[...] + pv

    if has_ctx:
        kc = jnp.concatenate([kvc_ref[:, 0:MLA_NOPE], krc_ref[...]], axis=1)
        softmax_pv(scores(kc), vtc_sc[...])
    if nchunks == 1:
        softmax_pv(scores(keys(0)), vt_sc[0])
    else:
        assert nchunks % 2 == 0
        qk_stage(0, slot0)
        qk_stage(1, slot1)
        softmax_stage(slot0)

        def body(i, carry):
            softmax_stage(slot1)
            qk_stage(2 * i + 2, slot0)
            pv_stage(2 * i, slot0)
            softmax_stage(slot0)
            qk_stage(2 * i + 3, slot1)
            pv_stage(2 * i + 1, slot1)
            return carry

        lax.fori_loop(0, nchunks // 2 - 1, body, 0)
        softmax_stage(slot1)
        pv_stage(nchunks - 2, slot0)
        pv_stage(nchunks - 1, slot1)
    o_ref[...] = (acc_sc[...] / l_sc[...]).T.astype(o_ref.dtype)


def _mla_attention(q, kv, kr, kv_c=None, kr_c=None):
    nq = q.shape[0]
    s = kv.shape[0]
    heads = q.shape[1] // MLA_Q_OUT
    tq = _pick(nq, (2048, 1024, 512, 256, 128))
    sub = _pick(tq, (512, 256, 128))
    tk = _pick(s, (512, 256, 128))
    has_ctx = kv_c is not None
    kvw = MLA_NOPE + MLA_V
    in_specs = [pl.BlockSpec((tq, MLA_Q_OUT), lambda h, i: (i, h)),
                pl.BlockSpec((s, kvw), lambda h, i: (0, h)),
                pl.BlockSpec((s, LANES), lambda h, i: (0, 0))]
    args = [q, kv, kr]
    scratch = [pltpu.VMEM((MLA_Q_OUT, tq), BF16), pltpu.VMEM((s // tk, MLA_V, tk), BF16)]
    if has_ctx:
        sc = kv_c.shape[0]
        in_specs += [pl.BlockSpec((sc, kvw), lambda h, i: (0, h)),
                     pl.BlockSpec((sc, LANES), lambda h, i: (0, 0))]
        args += [kv_c, kr_c]
        scratch.append(pltpu.VMEM((MLA_V, sc), BF16))
    slot = [pltpu.VMEM((tk, tq), F32), pltpu.VMEM((1, tq), F32), pltpu.VMEM((tk, tq), BF16), pltpu.VMEM((1, tq), F32)]
    scratch += slot + slot
    scratch += [pltpu.VMEM((1, tq), F32), pltpu.VMEM((1, tq), F32), pltpu.VMEM((MLA_V, tq), F32)]
    return pl.pallas_call(
        functools.partial(_mla_attn_kernel, tk=tk, sub=sub, has_ctx=has_ctx),
        out_shape=jax.ShapeDtypeStruct((nq, heads * MLA_V), BF16),
        grid=(heads, nq // tq),
        in_specs=in_specs,
        out_specs=pl.BlockSpec((tq, MLA_V), lambda h, i: (i, h)),
        scratch_shapes=scratch,
        compiler_params=_params("arbitrary", "arbitrary"),
        name="mla_attention",
    )(*args)


GLU_HALO = 16


def _glu_down_kernel(g_ref, v_ref, gp_ref, gn_ref, cw_ref, cb_ref, w_ref, o_ref):
    i = pl.program_id(0)
    k = pl.program_id(1)
    tm = g_ref.shape[0]

    @pl.when(k == 0)
    def _():
        o_ref[...] = jnp.zeros(o_ref.shape, F32)

    g = g_ref[...].astype(F32)
    prev_row = jnp.where(i > 0, gp_ref[GLU_HALO - 1:GLU_HALO, :].astype(F32), 0.0)
    next_row = jnp.where(i < pl.num_programs(0) - 1, gn_ref[0:1, :].astype(F32), 0.0)
    row = lax.broadcasted_iota(jnp.int32, (tm, 1), 0)
    g_dn = jnp.where(row == 0, prev_row, pltpu.roll(g, 1, axis=0))
    g_up = jnp.where(row == tm - 1, next_row, pltpu.roll(g, tm - 1, axis=0))
    cw = cw_ref[...]
    gc = g_dn * cw[0:1, :] + g * cw[1:2, :] + g_up * cw[2:3, :] + cb_ref[...]
    a = (gc * jax.nn.sigmoid(gc)) * v_ref[...].astype(F32)
    o_ref[...] = _dot(a.astype(BF16), w_ref[...]) + o_ref[...]


def _glu_down(u, conv_w, conv_b, w_down):
    n = u.shape[0]
    dff, d = w_down.shape
    tm = _pick(n, (512, 256, 128))
    tk = _pick(dff, (1024, 512, 256, 128))
    nk = dff // tk
    hb = tm // GLU_HALO
    nhb = n // GLU_HALO
    return pl.pallas_call(
        _glu_down_kernel,
        out_shape=jax.ShapeDtypeStruct((n, d), F32),
        grid=(n // tm, nk),
        in_specs=[pl.BlockSpec((tm, tk), lambda i, k: (i, k)),
                  pl.BlockSpec((tm, tk), lambda i, k: (i, nk + k)),
                  pl.BlockSpec((GLU_HALO, tk), lambda i, k: (jnp.maximum(i * hb - 1, 0), k)),
                  pl.BlockSpec((GLU_HALO, tk), lambda i, k: (jnp.minimum((i + 1) * hb, nhb - 1), k)),
                  pl.BlockSpec((3, tk), lambda i, k: (0, k)),
                  pl.BlockSpec((1, tk), lambda i, k: (0, k)),
                  pl.BlockSpec((tk, d), lambda i, k: (k, 0))],
        out_specs=pl.BlockSpec((tm, d), lambda i, k: (i, 0)),
        compiler_params=_params("arbitrary", "arbitrary"),
        name="glu_down",
    )(u, u, u, u, conv_w, conv_b, w_down)


def _conv_glu(h, w_up, conv_w, conv_b, w_down):
    return _glu_down(_mm(h, w_up, BF16), conv_w, conv_b, w_down)


def _ab_mixer(hl, hc, w_in, rpb, pool_w, pool_scale, w_out, need_ctx):
    n = hl.shape[0]
    naw = w_out.shape[0] // 2
    w_qkv = w_in[:, :3 * naw].astype(BF16)
    w_u = w_in[:, 3 * naw:].astype(BF16)
    w_out_a = w_out[:naw].astype(BF16)
    w_out_b = w_out[naw:].astype(BF16)
    pool_w = pool_w.astype(BF16)
    pool_scale = pool_scale.reshape(1, -1)
    qkv_l = _mm(hl, w_qkv, BF16)
    qkv_c = _mm(hc, w_qkv, BF16)
    o_a = _na_attention(qkv_l, qkv_c, _na_bias(rpb, n // GRID_W))
    o_b = _group_mm(_pool_diff(_mm(hl, w_u, F32)), pool_w, pool_scale)
    yl = _mm2(o_a, w_out_a, o_b, w_out_b, F32)
    yc = None
    if need_ctx:
        o_ac = _dense_attention(qkv_c)
        o_bc = _group_mm(_pool_diff(_mm(hc, w_u, F32)), pool_w, pool_scale)
        yc = _mm2(o_ac, w_out_a, o_bc, w_out_b, F32)
    return yl, yc


def _mla_mixer(hl, hc, w_down, g_q, g_kv, w_uq, w_ukv, w_out, tabs_l, tabs_c, need_ctx):
    qr = g_q.shape[0]
    kvr = g_kv.shape[0]
    heads = w_ukv.shape[1] // (MLA_NOPE + MLA_V)
    w_down_ext = jnp.concatenate([w_down[:, :qr + kvr], _rope_cols(w_down[:, qr + kvr:])], axis=1).astype(BF16)
    w_uq_h = w_uq.reshape(qr, heads, MLA_NOPE + MLA_ROPE)
    w_uq_ext = jnp.concatenate(
        [w_uq_h[..., :MLA_NOPE],
         _rope_cols(w_uq_h[..., MLA_NOPE:].reshape(qr * heads, MLA_ROPE)).reshape(qr, heads, 2 * LANES)],
        axis=-1).reshape(qr, heads * MLA_Q_IN).astype(BF16)
    w_ukv = w_ukv.astype(BF16)
    w_out = w_out.astype(BF16)
    g_q = g_q.reshape(1, -1)
    g_kv = g_kv.reshape(1, -1)
    qn_l, ckv_l, kr_l = _mla_down(hl, w_down_ext, g_q, g_kv, *tabs_l)
    qn_c, ckv_c, kr_c = _mla_down(hc, w_down_ext, g_q, g_kv, *tabs_c)
    q_l = _q_up(qn_l, w_uq_ext, *tabs_l, heads)
    kv_l = _mm(ckv_l, w_ukv, BF16)
    kv_c = _mm(ckv_c, w_ukv, BF16)
    yl = _mm(_mla_attention(q_l, kv_l, kr_l, kv_c, kr_c), w_out, F32)
    yc = None
    if need_ctx:
        q_c = _q_up(qn_c, w_uq_ext, *tabs_c, heads)
        yc = _mm(_mla_attention(q_c, kv_c, kr_c), w_out, F32)
    return yl, yc


def kernel(x, c, ctx, c_ctx, w_ada, b_ada, g_mix_pre, g_mix_post, g_ffn_pre, g_ffn_post, ab_w_in, na_rpb, pool_w, pool_scale, ab_w_out, mla_w_down, mla_g_q, mla_g_kv, mla_w_uq, mla_w_ukv, mla_w_out, ffn_w_up, ffn_conv_w, ffn_conv_b, ffn_w_down):
    batch, n, d = x.shape
    lc = ctx.shape[1]
    depth = w_ada.shape[0]
    assert batch == 1 and c.shape[0] == 1
    xl = x.reshape(n, d)
    xc = ctx.reshape(lc, d)

    cvec = jnp.concatenate([c, c_ctx[None, :], jnp.zeros((6, d), F32)], axis=0)
    mod = _ada(cvec, w_ada, b_ada)

    def mods(l, r):
        return [mod[l, r:r + 1, j * d:(j + 1) * d] for j in range(6)]

    tabs_l = _rope_tables(n)
    tabs_c = (jnp.concatenate([jnp.ones((lc, MLA_ROPE), F32), jnp.zeros((lc, LANES - MLA_ROPE), F32)], axis=1),
              jnp.zeros((lc, LANES), F32))

    vec = lambda a: a.reshape(1, -1)
    sh1, sc1, _, _, _, _ = mods(0, 0)
    sh1c, sc1c, _, _, _, _ = mods(0, 1)
    hl = _norm_mod(xl, vec(g_mix_pre[0]), sc1, sh1)
    hc = _norm_mod(xc, vec(g_mix_pre[0]), sc1c, sh1c)
    for l in range(depth):
        last = l == depth - 1
        sh1, sc1, gt1, sh2, sc2, gt2 = mods(l, 0)
        sh1c, sc1c, gt1c, sh2c, sc2c, gt2c = mods(l, 1)
        if l % 2 == 0:
            e = l // 2
            yl, yc = _ab_mixer(hl, hc, ab_w_in[e], na_rpb[e], pool_w[e], pool_scale[e], ab_w_out[e], not last)
        else:
            o = l // 2
            yl, yc = _mla_mixer(hl, hc, mla_w_down[o], mla_g_q[o], mla_g_kv[o], mla_w_uq[o], mla_w_ukv[o],
                                mla_w_out[o], tabs_l, tabs_c, not last)
        w_up = ffn_w_up[l].astype(BF16)
        w_dn = ffn_w_down[l].astype(BF16)
        conv_b = vec(ffn_conv_b[l])
        xl, h2 = _resid(xl, yl, vec(g_mix_post[l]), gt1, (vec(g_ffn_pre[l]), sc2, sh2))
        f = _conv_glu(h2, w_up, ffn_conv_w[l], conv_b, w_dn)
        if last:
            xl, _ = _resid(xl, f, vec(g_ffn_post[l]), gt2)
        else:
            nsh1, nsc1, _, _, _, _ = mods(l + 1, 0)
            xl, hl = _resid(xl, f, vec(g_ffn_post[l]), gt2, (vec(g_mix_pre[l + 1]), nsc1, nsh1))
            xc, h2c = _resid(xc, yc, vec(g_mix_post[l]), gt1c, (vec(g_ffn_pre[l]), sc2c, sh2c))
            fc = _conv_glu(h2c, w_up, ffn_conv_w[l], conv_b, w_dn)
            nsh1c, nsc1c, _, _, _, _ = mods(l + 1, 1)
            xc, hc = _resid(xc, fc, vec(g_ffn_post[l]), gt2c, (vec(g_mix_pre[l + 1]), nsc1c, nsh1c))
    return xl.reshape(batch, n, d)
```

```python
import functools

import numpy as np
import jax
import jax.numpy as jnp
from jax import lax
from jax.experimental import pallas as pl
from jax.experimental.pallas import tpu as pltpu

GRID_W = 64
EPS = 1e-6
NA_HEAD_DIM = 128
NA_WIN_R = 8
NA_WIN_C = 16
NA_Q_ROWS = 4
NA_K_ROWS = NA_Q_ROWS + NA_WIN_R
POOL_WINDOWS = (2, 4, 8, 16)
POOL_HALO = 8
MLA_NOPE = 128
MLA_ROPE = 64
MLA_V = 128
ROPE_THETA = 10000.0
LANES = 128
NEG = -1e30
LOG2E = 1.4426950408889634
VMEM_LIMIT = 56 * 1024 * 1024

F32 = jnp.float32
BF16 = jnp.bfloat16


def _pick(n, prefs):
    for p in prefs:
        if n % p == 0:
            return p
    return n


def _params(*sem):
    return pltpu.CompilerParams(dimension_semantics=sem, vmem_limit_bytes=VMEM_LIMIT)


def _rms(x):
    return x * lax.rsqrt(jnp.mean(x * x, axis=-1, keepdims=True) + EPS)


def _dot(a, b):
    return jnp.dot(a, b, preferred_element_type=F32)


def _dot_nt(a, b):
    return lax.dot_general(a, b, (((1,), (1,)), ((), ())), preferred_element_type=F32)


def _ada_kernel(c_ref, w_ref, b_ref, o_ref):
    c = c_ref[...]
    s = (c * jax.nn.sigmoid(c)).astype(BF16)
    o_ref[...] = _dot(s, w_ref[...].astype(BF16)) + b_ref[...]


def _ada(cvec, w_ada, b_ada):
    depth, d, n6 = w_ada.shape
    rows = cvec.shape[0]
    tn = _pick(n6, (512, 256, 128))
    return pl.pallas_call(
        _ada_kernel,
        out_shape=jax.ShapeDtypeStruct((depth, rows, n6), F32),
        grid=(depth, n6 // tn),
        in_specs=[
            pl.BlockSpec((rows, d), lambda l, j: (0, 0)),
            pl.BlockSpec((None, d, tn), lambda l, j: (l, 0, j)),
            pl.BlockSpec((None, 1, tn), lambda l, j: (l, 0, j)),
        ],
        out_specs=pl.BlockSpec((None, rows, tn), lambda l, j: (l, 0, j)),
        compiler_params=_params("arbitrary", "arbitrary"),
        name="ada",
    )(cvec, w_ada, b_ada.reshape(depth, 1, n6))


def _norm_mod_kernel(x_ref, g_ref, sc_ref, sh_ref, o_ref):
    y = _rms(x_ref[...]) * g_ref[...]
    o_ref[...] = (y * (1.0 + sc_ref[...]) + sh_ref[...]).astype(o_ref.dtype)


def _norm_mod(x, g, sc, sh):
    m, d = x.shape
    tm = _pick(m, (256, 128, 64, 32, 16))
    vec = pl.BlockSpec((1, d), lambda i: (0, 0))
    return pl.pallas_call(
        _norm_mod_kernel,
        out_shape=jax.ShapeDtypeStruct((m, d), BF16),
        grid=(m // tm,),
        in_specs=[pl.BlockSpec((tm, d), lambda i: (i, 0)), vec, vec, vec],
        out_specs=pl.BlockSpec((tm, d), lambda i: (i, 0)),
        compiler_params=_params("arbitrary"),
        name="norm_mod",
    )(x, g, sc, sh)


def _resid_kernel(x_ref, y_ref, gpost_ref, gt_ref, *rest, with_h):
    xn = x_ref[...] + gt_ref[...] * (_rms(y_ref[...]) * gpost_ref[...])
    if with_h:
        gpre_ref, sc_ref, sh_ref, xo_ref, ho_ref = rest
        xo_ref[...] = xn
        h = _rms(xn) * gpre_ref[...]
        ho_ref[...] = (h * (1.0 + sc_ref[...]) + sh_ref[...]).astype(ho_ref.dtype)
    else:
        (xo_ref,) = rest
        xo_ref[...] = xn


def _resid(x, y, gpost, gt, pre=None):
    m, d = x.shape
    tm = _pick(m, (256, 128, 64, 32, 16))
    vec = pl.BlockSpec((1, d), lambda i: (0, 0))
    row = pl.BlockSpec((tm, d), lambda i: (i, 0))
    with_h = pre is not None
    out_shape = [jax.ShapeDtypeStruct((m, d), F32)]
    out_specs = [row]
    args = [x, y, gpost, gt]
    in_specs = [row, row, vec, vec]
    if with_h:
        out_shape.append(jax.ShapeDtypeStruct((m, d), BF16))
        out_specs.append(row)
        args += list(pre)
        in_specs += [vec, vec, vec]
    out = pl.pallas_call(
        functools.partial(_resid_kernel, with_h=with_h),
        out_shape=out_shape,
        grid=(m // tm,),
        in_specs=in_specs,
        out_specs=out_specs,
        compiler_params=_params("arbitrary"),
        name="resid",
    )(*args)
    return (out[0], out[1]) if with_h else (out[0], None)


def _mm_kernel(a_ref, w_ref, o_ref):
    o_ref[...] = _dot(a_ref[...], w_ref[...]).astype(o_ref.dtype)


def _mm(a, w, out_dtype):
    m, k = a.shape
    n = w.shape[1]
    tm = _pick(m, (1024, 1280, 512, 256, 128))
    tn = _pick(n, (1024, 512, 256, 128))
    return pl.pallas_call(
        _mm_kernel,
        out_shape=jax.ShapeDtypeStruct((m, n), out_dtype),
        grid=(m // tm, n // tn),
        in_specs=[pl.BlockSpec((tm, k), lambda i, j: (i, 0)),
                  pl.BlockSpec((k, tn), lambda i, j: (0, j))],
        out_specs=pl.BlockSpec((tm, tn), lambda i, j: (i, j)),
        compiler_params=_params("arbitrary", "arbitrary"),
        name="mm",
    )(a, w)


def _mm2_kernel(a1_ref, w1_ref, a2_ref, w2_ref, o_ref):
    o_ref[...] = (_dot(a1_ref[...], w1_ref[...]) + _dot(a2_ref[...], w2_ref[...])).astype(o_ref.dtype)


def _mm2(a1, w1, a2, w2, out_dtype):
    m, k1 = a1.shape
    k2 = a2.shape[1]
    n = w1.shape[1]
    tm = _pick(m, (1024, 512, 256, 128))
    tn = _pick(n, (1024, 512, 256, 128))
    return pl.pallas_call(
        _mm2_kernel,
        out_shape=jax.ShapeDtypeStruct((m, n), out_dtype),
        grid=(m // tm, n // tn),
        in_specs=[pl.BlockSpec((tm, k1), lambda i, j: (i, 0)),
                  pl.BlockSpec((k1, tn), lambda i, j: (0, j)),
                  pl.BlockSpec((tm, k2), lambda i, j: (i, 0)),
                  pl.BlockSpec((k2, tn), lambda i, j: (0, j))],
        out_specs=pl.BlockSpec((tm, tn), lambda i, j: (i, j)),
        compiler_params=_params("arbitrary", "arbitrary"),
        name="mm2",
    )(a1, w1, a2, w2)


def _na_bias(rpb, rows):
    w = GRID_W
    h = rpb.shape[0]
    qc = np.arange(w)[:, None]
    kc = np.arange(w)[None, :]
    c0 = np.clip(qc - NA_WIN_C // 2, 0, w - NA_WIN_C)
    ok_c = (kc >= c0) & (kc < c0 + NA_WIN_C)
    dc = np.clip(kc - qc + NA_WIN_C - 1, 0, 2 * NA_WIN_C - 2)
    onehot = (dc[None] == np.arange(2 * NA_WIN_C - 1)[:, None, None]).astype(np.float32)
    t = jnp.einsum('hdj,jqk->hdqk', rpb.astype(F32), onehot, precision=lax.Precision.HIGHEST)
    t = jnp.where(ok_c, t, NEG)
    masked = jnp.full((h, w, w), NEG, F32)
    out = []
    for r_first, k_first in ((0, 0), (NA_Q_ROWS, 0), (rows - NA_Q_ROWS, rows - NA_K_ROWS)):
        q_rows = []
        for qr in range(NA_Q_ROWS):
            r = r_first + qr
            r0 = min(max(r - NA_WIN_R // 2, 0), rows - NA_WIN_R)
            blocks = []
            for kr in range(NA_K_ROWS):
                ka = k_first + kr
                blocks.append(t[:, ka - r + NA_WIN_R - 1] if r0 <= ka < r0 + NA_WIN_R else masked)
            q_rows.append(jnp.concatenate(blocks, axis=2))
        out.append(jnp.concatenate(q_rows, axis=1))
    return jnp.stack(out)


def _na_kernel(q_ref, k0_ref, k1_ref, k2_ref, v0_ref, v1_ref, v2_ref, kc_ref, vc_ref, b_ref, o_ref,
               *, scale, heads):
    qb = q_ref.shape[0]
    hd = [slice(h * NA_HEAD_DIM, (h + 1) * NA_HEAD_DIM) for h in range(heads)]
    qs = [q_ref[:, c] for c in hd]
    s_loc = [jnp.concatenate([_dot_nt(q, k_ref[:, c]) for k_ref in (k0_ref, k1_ref, k2_ref)], axis=1)
             * scale + b_ref[h] for h, (q, c) in enumerate(zip(qs, hd))]
    s_ctx = [_dot_nt(q, kc_ref[:, c]) * scale for q, c in zip(qs, hd)]
    ms = [jnp.maximum(jnp.max(sl, axis=1, keepdims=True), jnp.max(sc, axis=1, keepdims=True))
          for sl, sc in zip(s_loc, s_ctx)]
    p_loc = [jnp.exp(sl - m) for sl, m in zip(s_loc, ms)]
    p_ctx = [jnp.exp(sc - m) for sc, m in zip(s_ctx, ms)]
    for h, c in enumerate(hd):
        denom = jnp.sum(p_loc[h], axis=1, keepdims=True) + jnp.sum(p_ctx[h], axis=1, keepdims=True)
        o = _dot(p_ctx[h].astype(BF16), vc_ref[:, c])
        for j, v_ref in enumerate((v0_ref, v1_ref, v2_ref)):
            o = o + _dot(p_loc[h][:, j * qb:(j + 1) * qb].astype(BF16), v_ref[:, c])
        o_ref[:, c] = (o / denom).astype(o_ref.dtype)


def _na_attention(qkv_l, qkv_c, bias):
    n = qkv_l.shape[0]
    lc = qkv_c.shape[0]
    width = qkv_l.shape[1] // 3
    heads = width // NA_HEAD_DIM
    qb = NA_Q_ROWS * GRID_W
    nblk = n // qb
    assert n % qb == 0 and nblk >= 3
    hb = _pick(heads, (2, 1))
    hg = heads // hb
    bw = hb * NA_HEAD_DIM

    def kv_spec(j, col0):
        return pl.BlockSpec((qb, bw), lambda h, i: (jnp.clip(i - 1, 0, nblk - 3) + j, col0 + h))

    def bias_map(h, i):
        return (jnp.where(i == 0, 0, jnp.where(i == nblk - 1, 2, 1)), h, 0, 0)

    return pl.pallas_call(
        functools.partial(_na_kernel, scale=NA_HEAD_DIM ** -0.5, heads=hb),
        out_shape=jax.ShapeDtypeStruct((n, width), BF16),
        grid=(hg, nblk),
        in_specs=[
            pl.BlockSpec((qb, bw), lambda h, i: (i, h)),
            kv_spec(0, hg), kv_spec(1, hg), kv_spec(2, hg),
            kv_spec(0, 2 * hg), kv_spec(1, 2 * hg), kv_spec(2, 2 * hg),
            pl.BlockSpec((lc, bw), lambda h, i: (0, hg + h)),
            pl.BlockSpec((lc, bw), lambda h, i: (0, 2 * hg + h)),
            pl.BlockSpec((None, hb, qb, 3 * qb), bias_map),
        ],
        out_specs=pl.BlockSpec((qb, bw), lambda h, i: (i, h)),
        compiler_params=_params("arbitrary", "arbitrary"),
        name="na_attention",
    )(qkv_l, qkv_l, qkv_l, qkv_l, qkv_l, qkv_l, qkv_l, qkv_c, qkv_c, bias)


def _dense_attn_kernel(q_ref, k_ref, v_ref, o_ref, *, scale):
    s = _dot_nt(q_ref[...], k_ref[...]) * scale
    p = jnp.exp(s - jnp.max(s, axis=1, keepdims=True))
    o = _dot(p.astype(BF16), v_ref[...])
    o_ref[...] = (o / jnp.sum(p, axis=1, keepdims=True)).astype(o_ref.dtype)


def _dense_attention(qkv):
    n = qkv.shape[0]
    width = qkv.shape[1] // 3
    heads = width // NA_HEAD_DIM
    return pl.pallas_call(
        functools.partial(_dense_attn_kernel, scale=NA_HEAD_DIM ** -0.5),
        out_shape=jax.ShapeDtypeStruct((n, width), BF16),
        grid=(heads,),
        in_specs=[pl.BlockSpec((n, NA_HEAD_DIM), lambda h: (0, h)),
                  pl.BlockSpec((n, NA_HEAD_DIM), lambda h: (0, heads + h)),
                  pl.BlockSpec((n, NA_HEAD_DIM), lambda h: (0, 2 * heads + h))],
        out_specs=pl.BlockSpec((n, NA_HEAD_DIM), lambda h: (0, h)),
        compiler_params=_params("arbitrary"),
        name="dense_attention",
    )(qkv, qkv, qkv)


def _pool_kernel(u_ref, o_ref, *, n, chunk, blocks_per_group):
    group = pl.program_id(0) // blocks_per_group
    rows = chunk + 2 * POOL_HALO

    def window_sum(xs, w):
        a = pltpu.roll(xs, 1, axis=0) + xs
        half = 1
        while 2 * half < w:
            a = pltpu.roll(a, half, axis=0) + pltpu.roll(a, rows - half, axis=0)
            half *= 2
        return a

    for gi, w in enumerate(POOL_WINDOWS):
        @pl.when(group == gi)
        def _(w=w):
            def body(c, carry):
                base = pl.multiple_of(c * chunk, chunk)
                xs = u_ref[pl.ds(base, rows), :]
                tot = window_sum(xs, w)[POOL_HALO:POOL_HALO + chunk]
                t = base + lax.broadcasted_iota(jnp.int32, (chunk, 1), 0)
                cnt = jnp.minimum(t + w // 2, n) - jnp.maximum(t - w // 2, 0)
                mean = tot / cnt.astype(F32)
                o_ref[pl.ds(base, chunk), :] = (mean - xs[POOL_HALO:POOL_HALO + chunk]).astype(o_ref.dtype)
                return carry
            lax.fori_loop(0, n // chunk, body, 0)


def _pool_diff(u):
    n, c = u.shape
    cg = c // len(POOL_WINDOWS)
    assert cg % LANES == 0
    chunk = _pick(n, (512, 256, 128))
    up = jnp.pad(u, ((POOL_HALO, POOL_HALO), (0, 0)))
    return pl.pallas_call(
        functools.partial(_pool_kernel, n=n, chunk=chunk, blocks_per_group=cg // LANES),
        out_shape=jax.ShapeDtypeStruct((n, c), BF16),
        grid=(c // LANES,),
        in_specs=[pl.BlockSpec((n + 2 * POOL_HALO, LANES), lambda j: (0, j))],
        out_specs=pl.BlockSpec((n, LANES), lambda j: (0, j)),
        compiler_params=_params("arbitrary"),
        name="pool_diff",
    )(up)


def _group_mm_kernel(a_ref, w_ref, s_ref, o_ref):
    o_ref[...] = (_dot(a_ref[...], w_ref[...]) * s_ref[...]).astype(o_ref.dtype)


def _group_mm(d, w_grp, ch_scale):
    n, c = d.shape
    g, cg, _ = w_grp.shape
    tm = _pick(n, (1024, 512, 256, 128))
    return pl.pallas_call(
        _group_mm_kernel,
        out_shape=jax.ShapeDtypeStruct((n, c), BF16),
        grid=(g, n // tm),
        in_specs=[pl.BlockSpec((tm, cg), lambda gi, i: (i, gi)),
                  pl.BlockSpec((None, cg, cg), lambda gi, i: (gi, 0, 0)),
                  pl.BlockSpec((1, cg), lambda gi, i: (0, gi))],
        out_specs=pl.BlockSpec((tm, cg), lambda gi, i: (i, gi)),
        compiler_params=_params("arbitrary", "arbitrary"),
        name="group_mm",
    )(d, w_grp, ch_scale)


def _rope_tables(n):
    nf = MLA_ROPE // 4
    t = jnp.arange(n)
    row = (t // GRID_W).astype(F32)
    col = (t % GRID_W).astype(F32)
    freqs = ROPE_THETA ** (-jnp.arange(nf, dtype=F32) / nf)
    ang = jnp.stack([row[:, None] * freqs, col[:, None] * freqs], axis=1)
    cos, sin = jnp.cos(ang), jnp.sin(ang)
    c = jnp.stack([cos, cos], axis=2).reshape(n, MLA_ROPE)
    s = jnp.stack([-sin, sin], axis=2).reshape(n, MLA_ROPE)
    pad = jnp.zeros((n, LANES - MLA_ROPE), F32)
    return jnp.concatenate([c, pad], axis=1), jnp.concatenate([s, pad], axis=1)


def _rope_cols(w):
    nf = MLA_ROPE // 4
    j = jnp.arange(MLA_ROPE)
    partner = (j // (2 * nf)) * (2 * nf) + (1 - (j // nf) % 2) * nf + j % nf
    z = jnp.zeros((w.shape[0], LANES - MLA_ROPE), w.dtype)
    return jnp.concatenate([w, z, w[:, partner], z], axis=1)


def _mla_down_kernel(h_ref, w_ref, gq_ref, gkv_ref, c_ref, s_ref, qn_ref, ckv_ref, kr_ref, *, qr, kvr):
    z = _dot(h_ref[...], w_ref[...])
    qn_ref[...] = (_rms(z[:, :qr]) * gq_ref[...]).astype(qn_ref.dtype)
    ckv_ref[...] = (_rms(z[:, qr:qr + kvr]) * gkv_ref[...]).astype(ckv_ref.dtype)
    r0 = qr + kvr
    rot = z[:, r0:r0 + LANES] * c_ref[...] + z[:, r0 + LANES:r0 + 2 * LANES] * s_ref[...]
    kr_ref[...] = rot.astype(kr_ref.dtype)


def _mla_down(h, w_ext, g_q, g_kv, ctab, stab):
    n, d = h.shape
    qr = g_q.shape[1]
    kvr = g_kv.shape[1]
    nw = w_ext.shape[1]
    tm = _pick(n, (512, 256, 128))
    return pl.pallas_call(
        functools.partial(_mla_down_kernel, qr=qr, kvr=kvr),
        out_shape=[jax.ShapeDtypeStruct((n, qr), BF16),
                   jax.ShapeDtypeStruct((n, kvr), BF16),
                   jax.ShapeDtypeStruct((n, LANES), BF16)],
        grid=(n // tm,),
        in_specs=[pl.BlockSpec((tm, d), lambda i: (i, 0)),
                  pl.BlockSpec((d, nw), lambda i: (0, 0)),
                  pl.BlockSpec((1, qr), lambda i: (0, 0)),
                  pl.BlockSpec((1, kvr), lambda i: (0, 0)),
                  pl.BlockSpec((tm, LANES), lambda i: (i, 0)),
                  pl.BlockSpec((tm, LANES), lambda i: (i, 0))],
        out_specs=[pl.BlockSpec((tm, qr), lambda i: (i, 0)),
                   pl.BlockSpec((tm, kvr), lambda i: (i, 0)),
                   pl.BlockSpec((tm, LANES), lambda i: (i, 0))],
        compiler_params=_params("arbitrary"),
        name="mla_down",
    )(h, w_ext, g_q, g_kv, ctab, stab)


MLA_Q_IN = MLA_NOPE + 2 * LANES
MLA_Q_OUT = MLA_NOPE + LANES
MLA_VX = MLA_V + 16
MLA_JUMP_LIMIT = 64.0


def _q_up_kernel(a_ref, w_ref, c_ref, s_ref, o_ref, *, heads, scale):
    z = _dot(a_ref[...], w_ref[...])
    c = c_ref[...] * scale
    s = s_ref[...] * scale
    for j in range(heads):
        zi = j * MLA_Q_IN
        oi = j * MLA_Q_OUT
        o_ref[:, oi:oi + MLA_NOPE] = (z[:, zi:zi + MLA_NOPE] * scale).astype(o_ref.dtype)
        rot = z[:, zi + MLA_NOPE:zi + MLA_NOPE + LANES] * c + z[:, zi + MLA_NOPE + LANES:zi + MLA_Q_IN] * s
        o_ref[:, oi + MLA_NOPE:oi + MLA_Q_OUT] = rot.astype(o_ref.dtype)


def _q_up(qn, w_ext, ctab, stab, heads):
    n, qr = qn.shape
    tm = _pick(n, (1024, 512, 256, 128))
    hb = _pick(heads, (4, 2, 1))
    scale = (MLA_NOPE + MLA_ROPE) ** -0.5 * LOG2E
    return pl.pallas_call(
        functools.partial(_q_up_kernel, heads=hb, scale=scale),
        out_shape=jax.ShapeDtypeStruct((n, heads * MLA_Q_OUT), BF16),
        grid=(n // tm, heads // hb),
        in_specs=[pl.BlockSpec((tm, qr), lambda i, j: (i, 0)),
                  pl.BlockSpec((qr, hb * MLA_Q_IN), lambda i, j: (0, j)),
                  pl.BlockSpec((tm, LANES), lambda i, j: (i, 0)),
                  pl.BlockSpec((tm, LANES), lambda i, j: (i, 0))],
        out_specs=pl.BlockSpec((tm, hb * MLA_Q_OUT), lambda i, j: (i, j)),
        compiler_params=_params("arbitrary", "arbitrary"),
        name="mla_q_up",
    )(qn, w_ext, ctab, stab)


def _mla_attn_kernel(q_ref, kv_ref, kr_ref, *rest, tk, sub, has_ctx):
    if has_ctx:
        kvc_ref, krc_ref, o_ref, qt_sc, vt_sc, vtc_sc = rest[:6]
    else:
        o_ref, qt_sc, vt_sc = rest[:3]
    slot0, slot1 = rest[-14:-10], rest[-10:-6]
    m_sc, l_sc, acc_sc, accx_sc, ref_sc, jump_sc = rest[-6:]
    tq = q_ref.shape[0]
    nchunks = kv_ref.shape[0] // tk
    cols = [slice(j * sub, (j + 1) * sub) for j in range(tq // sub)]

    @pl.when(pl.program_id(1) == 0)
    def _():
        def tr(c, carry):
            ks = pl.multiple_of(c * tk, tk)
            vt_sc[c, 0:MLA_V, :] = kv_ref[pl.ds(ks, tk), MLA_NOPE:MLA_NOPE + MLA_V].T
            vt_sc[c, MLA_V:MLA_VX, :] = jnp.ones((MLA_VX - MLA_V, tk), BF16)
            return carry
        lax.fori_loop(0, nchunks, tr, 0)
        if has_ctx:
            vtc_sc[0:MLA_V, :] = kvc_ref[:, MLA_NOPE:MLA_NOPE + MLA_V].T
            vtc_sc[MLA_V:MLA_VX, :] = jnp.ones((MLA_VX - MLA_V, vtc_sc.shape[1]), BF16)

    qt_sc[...] = q_ref[...].T

    def keys(c):
        ks = pl.multiple_of(c * tk, tk)
        return jnp.concatenate([kv_ref[pl.ds(ks, tk), 0:MLA_NOPE], kr_ref[pl.ds(ks, tk), :]], axis=1)

    def ctx_keys():
        return jnp.concatenate([kvc_ref[:, 0:MLA_NOPE], krc_ref[...]], axis=1)

    def scores(k):
        return [_dot(k, qt_sc[:, c]) for c in cols]

    def fast_pass():
        sts = scores(ctx_keys())
        m0 = jnp.concatenate([jnp.max(st, axis=0, keepdims=True) for st in sts], axis=1)
        vtc = vtc_sc[...]
        accx_sc[...] = jnp.concatenate(
            [_dot(vtc, jnp.exp2(st - m0[:, c]).astype(BF16)) for st, c in zip(sts, cols)], axis=1)
        m_sc[...] = m0
        ref_sc[...] = m0
        jump_sc[...] = jnp.zeros(jump_sc.shape, F32)

        def qke_stage(c, slot):
            p_sc, r_sc = slot[2], slot[3]
            m_cur = m_sc[...]
            k = keys(c)
            mxs = []
            for col in cols:
                st = _dot(k, qt_sc[:, col])
                p_sc[:, col] = jnp.exp2(st - m_cur[:, col]).astype(BF16)
                mxs.append(jnp.max(st, axis=0, keepdims=True))
            mx = jnp.concatenate(mxs, axis=1)
            r_sc[...] = m_cur
            jump_sc[...] = jnp.maximum(jump_sc[...], mx - m_cur)
            m_sc[...] = jnp.maximum(m_cur, mx)

        def pvx_stage(c, slot):
            p_sc, r_sc = slot[2], slot[3]
            vt = vt_sc[c]
            pv = jnp.concatenate([_dot(vt, p_sc[:, col]) for col in cols], axis=1)
            r = r_sc[...]
            accx_sc[...] = accx_sc[...] * jnp.exp2(ref_sc[...] - r) + pv
            ref_sc[...] = r

        qke_stage(0, slot0)

        def body(i, carry):
            qke_stage(2 * i + 1, slot1)
            pvx_stage(2 * i, slot0)
            qke_stage(2 * i + 2, slot0)
            pvx_stage(2 * i + 1, slot1)
            return carry

        lax.fori_loop(0, nchunks // 2 - 1, body, 0)
        qke_stage(nchunks - 1, slot1)
        pvx_stage(nchunks - 2, slot0)
        pvx_stage(nchunks - 1, slot1)
        accx = accx_sc[...]
        o_ref[...] = (accx[0:MLA_V] / accx[MLA_V:MLA_V + 1]).T.astype(o_ref.dtype)

    if has_ctx and nchunks > 1:
        assert nchunks % 2 == 0
        fast_pass()
        needs_exact = jnp.max(jump_sc[...]) > MLA_JUMP_LIMIT
    else:
        needs_exact = None

    @pl.when(True if needs_exact is None else needs_exact)
    def _():
        _mla_exact_pass(o_ref, kv_ref, kr_ref, kvc_ref if has_ctx else None, krc_ref if has_ctx else None,
                        qt_sc, vt_sc, vtc_sc if has_ctx else None, slot0, slot1, m_sc, l_sc, acc_sc,
                        tk=tk, cols=cols)


def _mla_exact_pass(o_ref, kv_ref, kr_ref, kvc_ref, krc_ref, qt_sc, vt_sc, vtc_sc, slot0, slot1,
                    m_sc, l_sc, acc_sc, *, tk, cols):
    has_ctx = kvc_ref is not None
    nchunks = kv_ref.shape[0] // tk
    assert nchunks == 1 or nchunks % 2 == 0
    m_sc[...] = jnp.full(m_sc.shape, NEG, F32)
    l_sc[...] = jnp.zeros(l_sc.shape, F32)
    acc_sc[...] = jnp.zeros(acc_sc.shape, F32)

    def keys(c):
        ks = pl.multiple_of(c * tk, tk)
        return jnp.concatenate([kv_ref[pl.ds(ks, tk), 0:MLA_NOPE], kr_ref[pl.ds(ks, tk), :]], axis=1)

    def scores(k):
        return [_dot(k, qt_sc[:, c]) for c in cols]

    def softmax_pv(sts, vt):
        m_prev = m_sc[...]
        m_new = jnp.maximum(m_prev, jnp.concatenate(
            [jnp.max(st, axis=0, keepdims=True) for st in sts], axis=1))
        alpha = jnp.exp2(m_prev - m_new)
        sums, pvs = [], []
        for st, c in zip(sts, cols):
            p = jnp.exp2(st - m_new[:, c])
            sums.append(jnp.sum(p, axis=0, keepdims=True))
            pvs.append(_dot(vt, p.astype(BF16)))
        l_sc[...] = alpha * l_sc[...] + jnp.concatenate(sums, axis=1)
        acc_sc[...] = alpha * acc_sc[...] + jnp.concatenate(pvs, axis=1)
        m_sc[...] = m_new

    def qk_stage(c, slot):
        st_sc, mx_sc = slot[0], slot[1]
        for col, st in zip(cols, scores(keys(c))):
            st_sc[:, col] = st
            mx_sc[:, col] = jnp.max(st, axis=0, keepdims=True)

    def softmax_stage(slot):
        st_sc, mx_sc, p_sc, al_sc = slot
        m_prev = m_sc[...]
        m_new = jnp.maximum(m_prev, mx_sc[...])
        alpha = jnp.exp2(m_prev - m_new)
        sums = []
        for col in cols:
            p = jnp.exp2(st_sc[:, col] - m_new[:, col])
            sums.append(jnp.sum(p, axis=0, keepdims=True))
            p_sc[:, col] = p.astype(BF16)
        l_sc[...] = alpha * l_sc[...] + jnp.concatenate(sums, axis=1)
        al_sc[...] = alpha
        m_sc[...] = m_new

    def pv_stage(c, slot):
        p_sc, al_sc = slot[2], slot[3]
        vt = vt_sc[c, 0:MLA_V, :]
        pv = jnp.concatenate([_dot(vt, p_sc[:, col]) for col in cols], axis=1)
        acc_sc[...] = al_sc[...] * acc_sc[...] + pv

    if has_ctx:
        kc = jnp.concatenate([kvc_ref[:, 0:MLA_NOPE], krc_ref[...]], axis=1)
        softmax_pv(scores(kc), vtc_sc[0:MLA_V, :])
    if nchunks == 1:
        softmax_pv(scores(keys(0)), vt_sc[0, 0:MLA_V, :])
    else:
        qk_stage(0, slot0)
        qk_stage(1, slot1)
        softmax_stage(slot0)

        def body(i, carry):
            softmax_stage(slot1)
            qk_stage(2 * i + 2, slot0)
            pv_stage(2 * i, slot0)
            softmax_stage(slot0)
            qk_stage(2 * i + 3, slot1)
            pv_stage(2 * i + 1, slot1)
            return carry

        lax.fori_loop(0, nchunks // 2 - 1, body, 0)
        softmax_stage(slot1)
        pv_stage(nchunks - 2, slot0)
        pv_stage(nchunks - 1, slot1)
    o_ref[...] = (acc_sc[...] / l_sc[...]).T.astype(o_ref.dtype)


def _mla_attention(q, kv, kr, kv_c=None, kr_c=None):
    nq = q.shape[0]
    s = kv.shape[0]
    heads = q.shape[1] // MLA_Q_OUT
    tq = _pick(nq, (2048, 1024, 512, 256, 128))
    sub = _pick(tq, (512, 256, 128))
    tk = _pick(s, (512, 256, 128))
    has_ctx = kv_c is not None
    kvw = MLA_NOPE + MLA_V
    in_specs = [pl.BlockSpec((tq, MLA_Q_OUT), lambda h, i: (i, h)),
                pl.BlockSpec((s, kvw), lambda h, i: (0, h)),
                pl.BlockSpec((s, LANES), lambda h, i: (0, 0))]
    args = [q, kv, kr]
    scratch = [pltpu.VMEM((MLA_Q_OUT, tq), BF16), pltpu.VMEM((s // tk, MLA_VX, tk), BF16)]
    if has_ctx:
        sc = kv_c.shape[0]
        in_specs += [pl.BlockSpec((sc, kvw), lambda h, i: (0, h)),
                     pl.BlockSpec((sc, LANES), lambda h, i: (0, 0))]
        args += [kv_c, kr_c]
        scratch.append(pltpu.VMEM((MLA_VX, sc), BF16))
    slot = [pltpu.VMEM((tk, tq), F32), pltpu.VMEM((1, tq), F32), pltpu.VMEM((tk, tq), BF16), pltpu.VMEM((1, tq), F32)]
    scratch += slot + slot
    scratch += [pltpu.VMEM((1, tq), F32), pltpu.VMEM((1, tq), F32), pltpu.VMEM((MLA_V, tq), F32),
                pltpu.VMEM((MLA_VX, tq), F32), pltpu.VMEM((1, tq), F32), pltpu.VMEM((1, tq), F32)]
    return pl.pallas_call(
        functools.partial(_mla_attn_kernel, tk=tk, sub=sub, has_ctx=has_ctx),
        out_shape=jax.ShapeDtypeStruct((nq, heads * MLA_V), BF16),
        grid=(heads, nq // tq),
        in_specs=in_specs,
        out_specs=pl.BlockSpec((tq, MLA_V), lambda h, i: (i, h)),
        scratch_shapes=scratch,
        compiler_params=_params("arbitrary", "arbitrary"),
        name="mla_attention",
    )(*args)


GLU_HALO = 16


def _glu_down_kernel(g_ref, v_ref, gp_ref, gn_ref, cw_ref, cb_ref, w_ref, o_ref):
    i = pl.program_id(0)
    k = pl.program_id(1)
    tm = g_ref.shape[0]

    @pl.when(k == 0)
    def _():
        o_ref[...] = jnp.zeros(o_ref.shape, F32)

    g = g_ref[...].astype(F32)
    prev_row = jnp.where(i > 0, gp_ref[GLU_HALO - 1:GLU_HALO, :].astype(F32), 0.0)
    next_row = jnp.where(i < pl.num_programs(0) - 1, gn_ref[0:1, :].astype(F32), 0.0)
    row = lax.broadcasted_iota(jnp.int32, (tm, 1), 0)
    g_dn = jnp.where(row == 0, prev_row, pltpu.roll(g, 1, axis=0))
    g_up = jnp.where(row == tm - 1, next_row, pltpu.roll(g, tm - 1, axis=0))
    cw = cw_ref[...]
    gc = g_dn * cw[0:1, :] + g * cw[1:2, :] + g_up * cw[2:3, :] + cb_ref[...]
    a = (gc * jax.nn.sigmoid(gc)) * v_ref[...].astype(F32)
    o_ref[...] = _dot(a.astype(BF16), w_ref[...]) + o_ref[...]


def _glu_down(u, conv_w, conv_b, w_down):
    n = u.shape[0]
    dff, d = w_down.shape
    tm = _pick(n, (512, 256, 128))
    tk = _pick(dff, (1024, 512, 256, 128))
    nk = dff // tk
    hb = tm // GLU_HALO
    nhb = n // GLU_HALO
    return pl.pallas_call(
        _glu_down_kernel,
        out_shape=jax.ShapeDtypeStruct((n, d), F32),
        grid=(n // tm, nk),
        in_specs=[pl.BlockSpec((tm, tk), lambda i, k: (i, k)),
                  pl.BlockSpec((tm, tk), lambda i, k: (i, nk + k)),
                  pl.BlockSpec((GLU_HALO, tk), lambda i, k: (jnp.maximum(i * hb - 1, 0), k)),
                  pl.BlockSpec((GLU_HALO, tk), lambda i, k: (jnp.minimum((i + 1) * hb, nhb - 1), k)),
                  pl.BlockSpec((3, tk), lambda i, k: (0, k)),
                  pl.BlockSpec((1, tk), lambda i, k: (0, k)),
                  pl.BlockSpec((tk, d), lambda i, k: (k, 0))],
        out_specs=pl.BlockSpec((tm, d), lambda i, k: (i, 0)),
        compiler_params=_params("arbitrary", "arbitrary"),
        name="glu_down",
    )(u, u, u, u, conv_w, conv_b, w_down)


def _conv_glu(h, w_up, conv_w, conv_b, w_down):
    return _glu_down(_mm(h, w_up, BF16), conv_w, conv_b, w_down)


def _ab_mixer(hl, hc, w_in, rpb, pool_w, pool_scale, w_out, need_ctx):
    n = hl.shape[0]
    naw = w_out.shape[0] // 2
    w_qkv = w_in[:, :3 * naw].astype(BF16)
    w_u = w_in[:, 3 * naw:].astype(BF16)
    w_out_a = w_out[:naw].astype(BF16)
    w_out_b = w_out[naw:].astype(BF16)
    pool_w = pool_w.astype(BF16)
    pool_scale = pool_scale.reshape(1, -1)
    qkv_l = _mm(hl, w_qkv, BF16)
    qkv_c = _mm(hc, w_qkv, BF16)
    o_a = _na_attention(qkv_l, qkv_c, _na_bias(rpb, n // GRID_W))
    o_b = _group_mm(_pool_diff(_mm(hl, w_u, F32)), pool_w, pool_scale)
    yl = _mm2(o_a, w_out_a, o_b, w_out_b, F32)
    yc = None
    if need_ctx:
        o_ac = _dense_attention(qkv_c)
        o_bc = _group_mm(_pool_diff(_mm(hc, w_u, F32)), pool_w, pool_scale)
        yc = _mm2(o_ac, w_out_a, o_bc, w_out_b, F32)
    return yl, yc


def _mla_mixer(hl, hc, w_down, g_q, g_kv, w_uq, w_ukv, w_out, tabs_l, tabs_c, need_ctx):
    qr = g_q.shape[0]
    kvr = g_kv.shape[0]
    heads = w_ukv.shape[1] // (MLA_NOPE + MLA_V)
    w_down_ext = jnp.concatenate([w_down[:, :qr + kvr], _rope_cols(w_down[:, qr + kvr:])], axis=1).astype(BF16)
    w_uq_h = w_uq.reshape(qr, heads, MLA_NOPE + MLA_ROPE)
    w_uq_ext = jnp.concatenate(
        [w_uq_h[..., :MLA_NOPE],
         _rope_cols(w_uq_h[..., MLA_NOPE:].reshape(qr * heads, MLA_ROPE)).reshape(qr, heads, 2 * LANES)],
        axis=-1).reshape(qr, heads * MLA_Q_IN).astype(BF16)
    w_ukv = w_ukv.astype(BF16)
    w_out = w_out.astype(BF16)
    g_q = g_q.reshape(1, -1)
    g_kv = g_kv.reshape(1, -1)
    qn_l, ckv_l, kr_l = _mla_down(hl, w_down_ext, g_q, g_kv, *tabs_l)
    qn_c, ckv_c, kr_c = _mla_down(hc, w_down_ext, g_q, g_kv, *tabs_c)
    q_l = _q_up(qn_l, w_uq_ext, *tabs_l, heads)
    kv_l = _mm(ckv_l, w_ukv, BF16)
    kv_c = _mm(ckv_c, w_ukv, BF16)
    yl = _mm(_mla_attention(q_l, kv_l, kr_l, kv_c, kr_c), w_out, F32)
    yc = None
    if need_ctx:
        q_c = _q_up(qn_c, w_uq_ext, *tabs_c, heads)
        yc = _mm(_mla_attention(q_c, kv_c, kr_c), w_out, F32)
    return yl, yc


def kernel(x, c, ctx, c_ctx, w_ada, b_ada, g_mix_pre, g_mix_post, g_ffn_pre, g_ffn_post, ab_w_in, na_rpb, pool_w, pool_scale, ab_w_out, mla_w_down, mla_g_q, mla_g_kv, mla_w_uq, mla_w_ukv, mla_w_out, ffn_w_up, ffn_conv_w, ffn_conv_b, ffn_w_down):
    batch, n, d = x.shape
    lc = ctx.shape[1]
    depth = w_ada.shape[0]
    assert batch == 1 and c.shape[0] == 1
    xl = x.reshape(n, d)
    xc = ctx.reshape(lc, d)

    cvec = jnp.concatenate([c, c_ctx[None, :], jnp.zeros((6, d), F32)], axis=0)
    mod = _ada(cvec, w_ada, b_ada)

    def mods(l, r):
        return [mod[l, r:r + 1, j * d:(j + 1) * d] for j in range(6)]

    tabs_l = _rope_tables(n)
    tabs_c = (jnp.concatenate([jnp.ones((lc, MLA_ROPE), F32), jnp.zeros((lc, LANES - MLA_ROPE), F32)], axis=1),
              jnp.zeros((lc, LANES), F32))

    vec = lambda a: a.reshape(1, -1)
    sh1, sc1, _, _, _, _ = mods(0, 0)
    sh1c, sc1c, _, _, _, _ = mods(0, 1)
    hl = _norm_mod(xl, vec(g_mix_pre[0]), sc1, sh1)
    hc = _norm_mod(xc, vec(g_mix_pre[0]), sc1c, sh1c)
    for l in range(depth):
        last = l == depth - 1
        sh1, sc1, gt1, sh2, sc2, gt2 = mods(l, 0)
        sh1c, sc1c, gt1c, sh2c, sc2c, gt2c = mods(l, 1)
        if l % 2 == 0:
            e = l // 2
            yl, yc = _ab_mixer(hl, hc, ab_w_in[e], na_rpb[e], pool_w[e], pool_scale[e], ab_w_out[e], not last)
        else:
            o = l // 2
            yl, yc = _mla_mixer(hl, hc, mla_w_down[o], mla_g_q[o], mla_g_kv[o], mla_w_uq[o], mla_w_ukv[o],
                                mla_w_out[o], tabs_l, tabs_c, not last)
        w_up = ffn_w_up[l].astype(BF16)
        w_dn = ffn_w_down[l].astype(BF16)
        conv_b = vec(ffn_conv_b[l])
        xl, h2 = _resid(xl, yl, vec(g_mix_post[l]), gt1, (vec(g_ffn_pre[l]), sc2, sh2))
        f = _conv_glu(h2, w_up, ffn_conv_w[l], conv_b, w_dn)
        if last:
            xl, _ = _resid(xl, f, vec(g_ffn_post[l]), gt2)
        else:
            nsh1, nsc1, _, _, _, _ = mods(l + 1, 0)
            xl, hl = _resid(xl, f, vec(g_ffn_post[l]), gt2, (vec(g_mix_pre[l + 1]), nsc1, nsh1))
            xc, h2c = _resid(xc, yc, vec(g_mix_post[l]), gt1c, (vec(g_ffn_pre[l]), sc2c, sh2c))
            fc = _conv_glu(h2c, w_up, ffn_conv_w[l], conv_b, w_dn)
            nsh1c, nsc1c, _, _, _, _ = mods(l + 1, 1)
            xc, hc = _resid(xc, fc, vec(g_ffn_post[l]), gt2c, (vec(g_mix_pre[l + 1]), nsc1c, nsh1c))
    return xl.reshape(batch, n, d)
```

```python
import functools

import numpy as np
import jax
import jax.numpy as jnp
from jax import lax
from jax.experimental import pallas as pl
from jax.experimental.pallas import tpu as pltpu

GRID_W = 64
EPS = 1e-6
NA_HEAD_DIM = 128
NA_WIN_R = 8
NA_WIN_C = 16
NA_Q_ROWS = 4
NA_K_ROWS = NA_Q_ROWS + NA_WIN_R
POOL_WINDOWS = (2, 4, 8, 16)
POOL_HALO = 8
MLA_NOPE = 128
MLA_ROPE = 64
MLA_V = 128
ROPE_THETA = 10000.0
LANES = 128
NEG = -1e30
LOG2E = 1.4426950408889634
VMEM_LIMIT = 56 * 1024 * 1024

F32 = jnp.float32
BF16 = jnp.bfloat16


def _pick(n, prefs):
    for p in prefs:
        if n % p == 0:
            return p
    return n


def _params(*sem):
    return pltpu.CompilerParams(dimension_semantics=sem, vmem_limit_bytes=VMEM_LIMIT)


def _rms(x):
    return x * lax.rsqrt(jnp.mean(x * x, axis=-1, keepdims=True) + EPS)


def _dot(a, b):
    return jnp.dot(a, b, preferred_element_type=F32)


def _dot_nt(a, b):
    return lax.dot_general(a, b, (((1,), (1,)), ((), ())), preferred_element_type=F32)


def _ada_kernel(c_ref, w_ref, b_ref, o_ref):
    c = c_ref[...]
    s = (c * jax.nn.sigmoid(c)).astype(BF16)
    o_ref[...] = _dot(s, w_ref[...].astype(BF16)) + b_ref[...]


def _ada(cvec, w_ada, b_ada):
    depth, d, n6 = w_ada.shape
    rows = cvec.shape[0]
    tn = _pick(n6, (512, 256, 128))
    return pl.pallas_call(
        _ada_kernel,
        out_shape=jax.ShapeDtypeStruct((depth, rows, n6), F32),
        grid=(depth, n6 // tn),
        in_specs=[
            pl.BlockSpec((rows, d), lambda l, j: (0, 0)),
            pl.BlockSpec((None, d, tn), lambda l, j: (l, 0, j)),
            pl.BlockSpec((None, 1, tn), lambda l, j: (l, 0, j)),
        ],
        out_specs=pl.BlockSpec((None, rows, tn), lambda l, j: (l, 0, j)),
        compiler_params=_params("arbitrary", "arbitrary"),
        name="ada",
    )(cvec, w_ada, b_ada.reshape(depth, 1, n6))


def _norm_mod_kernel(x_ref, g_ref, sc_ref, sh_ref, o_ref):
    y = _rms(x_ref[...]) * g_ref[...]
    o_ref[...] = (y * (1.0 + sc_ref[...]) + sh_ref[...]).astype(o_ref.dtype)


def _norm_mod(x, g, sc, sh):
    m, d = x.shape
    tm = _pick(m, (256, 128, 64, 32, 16))
    vec = pl.BlockSpec((1, d), lambda i: (0, 0))
    return pl.pallas_call(
        _norm_mod_kernel,
        out_shape=jax.ShapeDtypeStruct((m, d), BF16),
        grid=(m // tm,),
        in_specs=[pl.BlockSpec((tm, d), lambda i: (i, 0)), vec, vec, vec],
        out_specs=pl.BlockSpec((tm, d), lambda i: (i, 0)),
        compiler_params=_params("arbitrary"),
        name="norm_mod",
    )(x, g, sc, sh)


def _resid_kernel(x_ref, y_ref, gpost_ref, gt_ref, *rest, with_h):
    xn = x_ref[...] + gt_ref[...] * (_rms(y_ref[...]) * gpost_ref[...])
    if with_h:
        gpre_ref, sc_ref, sh_ref, xo_ref, ho_ref = rest
        xo_ref[...] = xn
        h = _rms(xn) * gpre_ref[...]
        ho_ref[...] = (h * (1.0 + sc_ref[...]) + sh_ref[...]).astype(ho_ref.dtype)
    else:
        (xo_ref,) = rest
        xo_ref[...] = xn


def _resid(x, y, gpost, gt, pre=None):
    m, d = x.shape
    tm = _pick(m, (256, 128, 64, 32, 16))
    vec = pl.BlockSpec((1, d), lambda i: (0, 0))
    row = pl.BlockSpec((tm, d), lambda i: (i, 0))
    with_h = pre is not None
    out_shape = [jax.ShapeDtypeStruct((m, d), F32)]
    out_specs = [row]
    args = [x, y, gpost, gt]
    in_specs = [row, row, vec, vec]
    if with_h:
        out_shape.append(jax.ShapeDtypeStruct((m, d), BF16))
        out_specs.append(row)
        args += list(pre)
        in_specs += [vec, vec, vec]
    out = pl.pallas_call(
        functools.partial(_resid_kernel, with_h=with_h),
        out_shape=out_shape,
        grid=(m // tm,),
        in_specs=in_specs,
        out_specs=out_specs,
        compiler_params=_params("arbitrary"),
        name="resid",
    )(*args)
    return (out[0], out[1]) if with_h else (out[0], None)


def _mm_kernel(a_ref, w_ref, o_ref):
    o_ref[...] = _dot(a_ref[...], w_ref[...]).astype(o_ref.dtype)


def _mm(a, w, out_dtype, col0=0, ncols=None):
    m, k = a.shape
    n = w.shape[1] - col0 if ncols is None else ncols
    tm = _pick(m, (1024, 1280, 512, 256, 128))
    tn = _pick(n, (1024, 512, 256, 128))
    assert col0 % tn == 0
    jb = col0 // tn
    return pl.pallas_call(
        _mm_kernel,
        out_shape=jax.ShapeDtypeStruct((m, n), out_dtype),
        grid=(m // tm, n // tn),
        in_specs=[pl.BlockSpec((tm, k), lambda i, j: (i, 0)),
                  pl.BlockSpec((k, tn), lambda i, j: (0, jb + j))],
        out_specs=pl.BlockSpec((tm, tn), lambda i, j: (i, j)),
        compiler_params=_params("arbitrary", "arbitrary"),
        name="mm",
    )(a, w)


def _mm2_kernel(a1_ref, w1_ref, a2_ref, w2_ref, o_ref):
    o_ref[...] = (_dot(a1_ref[...], w1_ref[...]) + _dot(a2_ref[...], w2_ref[...])).astype(o_ref.dtype)


def _mm2(a1, a2, w, out_dtype):
    m, k1 = a1.shape
    k2 = a2.shape[1]
    n = w.shape[1]
    assert k1 == k2 and w.shape[0] == k1 + k2
    tm = _pick(m, (1024, 512, 256, 128))
    tn = _pick(n, (1024, 512, 256, 128))
    return pl.pallas_call(
        _mm2_kernel,
        out_shape=jax.ShapeDtypeStruct((m, n), out_dtype),
        grid=(m // tm, n // tn),
        in_specs=[pl.BlockSpec((tm, k1), lambda i, j: (i, 0)),
                  pl.BlockSpec((k1, tn), lambda i, j: (0, j)),
                  pl.BlockSpec((tm, k2), lambda i, j: (i, 0)),
                  pl.BlockSpec((k2, tn), lambda i, j: (1, j))],
        out_specs=pl.BlockSpec((tm, tn), lambda i, j: (i, j)),
        compiler_params=_params("arbitrary", "arbitrary"),
        name="mm2",
    )(a1, w, a2, w)


def _na_bias(rpb, rows):
    w = GRID_W
    h = rpb.shape[0]
    qc = np.arange(w)[:, None]
    kc = np.arange(w)[None, :]
    c0 = np.clip(qc - NA_WIN_C // 2, 0, w - NA_WIN_C)
    ok_c = (kc >= c0) & (kc < c0 + NA_WIN_C)
    dc = np.clip(kc - qc + NA_WIN_C - 1, 0, 2 * NA_WIN_C - 2)
    onehot = (dc[None] == np.arange(2 * NA_WIN_C - 1)[:, None, None]).astype(np.float32)
    t = jnp.einsum('hdj,jqk->hdqk', rpb.astype(F32), onehot, precision=lax.Precision.HIGHEST)
    t = jnp.where(ok_c, t, NEG)
    masked = jnp.full((h, w, w), NEG, F32)
    out = []
    for r_first, k_first in ((0, 0), (NA_Q_ROWS, 0), (rows - NA_Q_ROWS, rows - NA_K_ROWS)):
        q_rows = []
        for qr in range(NA_Q_ROWS):
            r = r_first + qr
            r0 = min(max(r - NA_WIN_R // 2, 0), rows - NA_WIN_R)
            blocks = []
            for kr in range(NA_K_ROWS):
                ka = k_first + kr
                blocks.append(t[:, ka - r + NA_WIN_R - 1] if r0 <= ka < r0 + NA_WIN_R else masked)
            q_rows.append(jnp.concatenate(blocks, axis=2))
        out.append(jnp.concatenate(q_rows, axis=1))
    return jnp.stack(out) * LOG2E


def _na_kernel(q_ref, k0_ref, k1_ref, k2_ref, v0_ref, v1_ref, v2_ref, kc_ref, vc_ref, b_ref, o_ref,
               *, scale, heads):
    qb = q_ref.shape[0]
    hd = [slice(h * NA_HEAD_DIM, (h + 1) * NA_HEAD_DIM) for h in range(heads)]
    qs = [q_ref[:, c] for c in hd]
    s_loc = [jnp.concatenate([_dot_nt(q, k_ref[:, c]) for k_ref in (k0_ref, k1_ref, k2_ref)], axis=1)
             * scale + b_ref[h] for h, (q, c) in enumerate(zip(qs, hd))]
    s_ctx = [_dot_nt(q, kc_ref[:, c]) * scale for q, c in zip(qs, hd)]
    ms = [jnp.maximum(jnp.max(sl, axis=1, keepdims=True), jnp.max(sc, axis=1, keepdims=True))
          for sl, sc in zip(s_loc, s_ctx)]
    p_loc = [jnp.exp2(sl - m) for sl, m in zip(s_loc, ms)]
    p_ctx = [jnp.exp2(sc - m) for sc, m in zip(s_ctx, ms)]
    for h, c in enumerate(hd):
        denom = jnp.sum(p_loc[h], axis=1, keepdims=True) + jnp.sum(p_ctx[h], axis=1, keepdims=True)
        o = _dot(p_ctx[h].astype(BF16), vc_ref[:, c])
        for j, v_ref in enumerate((v0_ref, v1_ref, v2_ref)):
            o = o + _dot(p_loc[h][:, j * qb:(j + 1) * qb].astype(BF16), v_ref[:, c])
        o_ref[:, c] = (o / denom).astype(o_ref.dtype)


def _na_attention(qkv_l, qkv_c, bias):
    n = qkv_l.shape[0]
    lc = qkv_c.shape[0]
    width = qkv_l.shape[1] // 3
    heads = width // NA_HEAD_DIM
    qb = NA_Q_ROWS * GRID_W
    nblk = n // qb
    assert n % qb == 0 and nblk >= 3
    hb = _pick(heads, (2, 1))
    hg = heads // hb
    bw = hb * NA_HEAD_DIM

    def kv_spec(j, col0):
        return pl.BlockSpec((qb, bw), lambda h, i: (jnp.clip(i - 1, 0, nblk - 3) + j, col0 + h))

    def bias_map(h, i):
        return (jnp.where(i == 0, 0, jnp.where(i == nblk - 1, 2, 1)), h, 0, 0)

    return pl.pallas_call(
        functools.partial(_na_kernel, scale=NA_HEAD_DIM ** -0.5 * LOG2E, heads=hb),
        out_shape=jax.ShapeDtypeStruct((n, width), BF16),
        grid=(hg, nblk),
        in_specs=[
            pl.BlockSpec((qb, bw), lambda h, i: (i, h)),
            kv_spec(0, hg), kv_spec(1, hg), kv_spec(2, hg),
            kv_spec(0, 2 * hg), kv_spec(1, 2 * hg), kv_spec(2, 2 * hg),
            pl.BlockSpec((lc, bw), lambda h, i: (0, hg + h)),
            pl.BlockSpec((lc, bw), lambda h, i: (0, 2 * hg + h)),
            pl.BlockSpec((None, hb, qb, 3 * qb), bias_map),
        ],
        out_specs=pl.BlockSpec((qb, bw), lambda h, i: (i, h)),
        compiler_params=_params("arbitrary", "arbitrary"),
        name="na_attention",
    )(qkv_l, qkv_l, qkv_l, qkv_l, qkv_l, qkv_l, qkv_l, qkv_c, qkv_c, bias)


def _dense_attn_kernel(q_ref, k_ref, v_ref, o_ref, *, scale):
    s = _dot_nt(q_ref[...], k_ref[...]) * scale
    p = jnp.exp(s - jnp.max(s, axis=1, keepdims=True))
    o = _dot(p.astype(BF16), v_ref[...])
    o_ref[...] = (o / jnp.sum(p, axis=1, keepdims=True)).astype(o_ref.dtype)


def _dense_attention(qkv):
    n = qkv.shape[0]
    width = qkv.shape[1] // 3
    heads = width // NA_HEAD_DIM
    return pl.pallas_call(
        functools.partial(_dense_attn_kernel, scale=NA_HEAD_DIM ** -0.5),
        out_shape=jax.ShapeDtypeStruct((n, width), BF16),
        grid=(heads,),
        in_specs=[pl.BlockSpec((n, NA_HEAD_DIM), lambda h: (0, h)),
                  pl.BlockSpec((n, NA_HEAD_DIM), lambda h: (0, heads + h)),
                  pl.BlockSpec((n, NA_HEAD_DIM), lambda h: (0, 2 * heads + h))],
        out_specs=pl.BlockSpec((n, NA_HEAD_DIM), lambda h: (0, h)),
        compiler_params=_params("arbitrary"),
        name="dense_attention",
    )(qkv, qkv, qkv)


def _pool_kernel(u_ref, o_ref, *, n, chunk, blocks_per_group):
    group = pl.program_id(0) // blocks_per_group
    rows = chunk + 2 * POOL_HALO

    def window_sum(xs, w):
        a = pltpu.roll(xs, 1, axis=0) + xs
        half = 1
        while 2 * half < w:
            a = pltpu.roll(a, half, axis=0) + pltpu.roll(a, rows - half, axis=0)
            half *= 2
        return a

    for gi, w in enumerate(POOL_WINDOWS):
        @pl.when(group == gi)
        def _(w=w):
            def body(c, carry):
                base = pl.multiple_of(c * chunk, chunk)
                xs = u_ref[pl.ds(base, rows), :]
                tot = window_sum(xs, w)[POOL_HALO:POOL_HALO + chunk]
                t = base + lax.broadcasted_iota(jnp.int32, (chunk, 1), 0)
                cnt = jnp.minimum(t + w // 2, n) - jnp.maximum(t - w // 2, 0)
                mean = tot / cnt.astype(F32)
                o_ref[pl.ds(base, chunk), :] = (mean - xs[POOL_HALO:POOL_HALO + chunk]).astype(o_ref.dtype)
                return carry
            lax.fori_loop(0, n // chunk, body, 0)


def _pool_diff(u):
    n, c = u.shape
    cg = c // len(POOL_WINDOWS)
    assert cg % LANES == 0
    chunk = _pick(n, (512, 256, 128))
    up = jnp.pad(u, ((POOL_HALO, POOL_HALO), (0, 0)))
    return pl.pallas_call(
        functools.partial(_pool_kernel, n=n, chunk=chunk, blocks_per_group=cg // LANES),
        out_shape=jax.ShapeDtypeStruct((n, c), BF16),
        grid=(c // LANES,),
        in_specs=[pl.BlockSpec((n + 2 * POOL_HALO, LANES), lambda j: (0, j))],
        out_specs=pl.BlockSpec((n, LANES), lambda j: (0, j)),
        compiler_params=_params("arbitrary"),
        name="pool_diff",
    )(up)


def _group_mm_kernel(a_ref, w_ref, s_ref, o_ref):
    o_ref[...] = (_dot(a_ref[...], w_ref[...]) * s_ref[...]).astype(o_ref.dtype)


def _group_mm(d, w_grp, ch_scale):
    n, c = d.shape
    g, cg, _ = w_grp.shape
    tm = _pick(n, (1024, 512, 256, 128))
    return pl.pallas_call(
        _group_mm_kernel,
        out_shape=jax.ShapeDtypeStruct((n, c), BF16),
        grid=(g, n // tm),
        in_specs=[pl.BlockSpec((tm, cg), lambda gi, i: (i, gi)),
                  pl.BlockSpec((None, cg, cg), lambda gi, i: (gi, 0, 0)),
                  pl.BlockSpec((1, cg), lambda gi, i: (0, gi))],
        out_specs=pl.BlockSpec((tm, cg), lambda gi, i: (i, gi)),
        compiler_params=_params("arbitrary", "arbitrary"),
        name="group_mm",
    )(d, w_grp, ch_scale)


def _rope_tables(n):
    nf = MLA_ROPE // 4
    t = jnp.arange(n)
    row = (t // GRID_W).astype(F32)
    col = (t % GRID_W).astype(F32)
    freqs = ROPE_THETA ** (-jnp.arange(nf, dtype=F32) / nf)
    ang = jnp.stack([row[:, None] * freqs, col[:, None] * freqs], axis=1)
    cos, sin = jnp.cos(ang), jnp.sin(ang)
    c = jnp.stack([cos, cos], axis=2).reshape(n, MLA_ROPE)
    s = jnp.stack([-sin, sin], axis=2).reshape(n, MLA_ROPE)
    pad = jnp.zeros((n, LANES - MLA_ROPE), F32)
    return jnp.concatenate([c, pad], axis=1), jnp.concatenate([s, pad], axis=1)


def _rope_partner():
    nf = MLA_ROPE // 4
    j = np.arange(MLA_ROPE)
    return (j // (2 * nf)) * (2 * nf) + (1 - (j // nf) % 2) * nf + j % nf


def _rope_cols(w):
    z = jnp.zeros((w.shape[0], LANES - MLA_ROPE), w.dtype)
    return jnp.concatenate([w, z, w[:, _rope_partner()], z], axis=1)


def _mla_down_kernel(h_ref, w_ref, gq_ref, gkv_ref, c_ref, s_ref, qn_ref, ckv_ref, kr_ref, *, qr, kvr):
    z = _dot(h_ref[...], w_ref[...])
    qn_ref[...] = (_rms(z[:, :qr]) * gq_ref[...]).astype(qn_ref.dtype)
    ckv_ref[...] = (_rms(z[:, qr:qr + kvr]) * gkv_ref[...]).astype(ckv_ref.dtype)
    r0 = qr + kvr
    rot = z[:, r0:r0 + LANES] * c_ref[...] + z[:, r0 + LANES:r0 + 2 * LANES] * s_ref[...]
    kr_ref[...] = rot.astype(kr_ref.dtype)


def _mla_down(h, w_ext, g_q, g_kv, ctab, stab):
    n, d = h.shape
    qr = g_q.shape[1]
    kvr = g_kv.shape[1]
    nw = w_ext.shape[1]
    tm = _pick(n, (512, 256, 128))
    return pl.pallas_call(
        functools.partial(_mla_down_kernel, qr=qr, kvr=kvr),
        out_shape=[jax.ShapeDtypeStruct((n, qr), BF16),
                   jax.ShapeDtypeStruct((n, kvr), BF16),
                   jax.ShapeDtypeStruct((n, LANES), BF16)],
        grid=(n // tm,),
        in_specs=[pl.BlockSpec((tm, d), lambda i: (i, 0)),
                  pl.BlockSpec((d, nw), lambda i: (0, 0)),
                  pl.BlockSpec((1, qr), lambda i: (0, 0)),
                  pl.BlockSpec((1, kvr), lambda i: (0, 0)),
                  pl.BlockSpec((tm, LANES), lambda i: (i, 0)),
                  pl.BlockSpec((tm, LANES), lambda i: (i, 0))],
        out_specs=[pl.BlockSpec((tm, qr), lambda i: (i, 0)),
                   pl.BlockSpec((tm, kvr), lambda i: (i, 0)),
                   pl.BlockSpec((tm, LANES), lambda i: (i, 0))],
        compiler_params=_params("arbitrary"),
        name="mla_down",
    )(h, w_ext, g_q, g_kv, ctab, stab)


MLA_Q_IN = MLA_NOPE + LANES
MLA_Q_OUT = MLA_NOPE + LANES
MLA_VX = MLA_V + 16
MLA_JUMP_LIMIT = 64.0


def _q_up_kernel(a_ref, w_ref, cs_ref, o_ref, *, heads, scale):
    z = _dot(a_ref[...], w_ref[...])
    cs = cs_ref[...] * scale
    for j in range(heads):
        zi = j * MLA_Q_IN
        oi = j * MLA_Q_OUT
        o_ref[:, oi:oi + MLA_NOPE] = (z[:, zi:zi + MLA_NOPE] * scale).astype(o_ref.dtype)
        t = z[:, zi + MLA_NOPE:zi + MLA_Q_IN] * cs
        rot = t + pltpu.roll(t, LANES // 2, axis=1)
        o_ref[:, oi + MLA_NOPE:oi + MLA_Q_OUT] = rot.astype(o_ref.dtype)


def _q_up(qn, w_ext, cstab, heads):
    n, qr = qn.shape
    tm = _pick(n, (1024, 512, 256, 128))
    hb = _pick(heads, (4, 2, 1))
    scale = (MLA_NOPE + MLA_ROPE) ** -0.5 * LOG2E
    return pl.pallas_call(
        functools.partial(_q_up_kernel, heads=hb, scale=scale),
        out_shape=jax.ShapeDtypeStruct((n, heads * MLA_Q_OUT), BF16),
        grid=(n // tm, heads // hb),
        in_specs=[pl.BlockSpec((tm, qr), lambda i, j: (i, 0)),
                  pl.BlockSpec((qr, hb * MLA_Q_IN), lambda i, j: (0, j)),
                  pl.BlockSpec((tm, LANES), lambda i, j: (i, 0))],
        out_specs=pl.BlockSpec((tm, hb * MLA_Q_OUT), lambda i, j: (i, j)),
        compiler_params=_params("arbitrary", "arbitrary"),
        name="mla_q_up",
    )(qn, w_ext, cstab)


def _mla_attn_kernel(q_ref, kv_ref, kr_ref, *rest, tk, sub, has_ctx):
    if has_ctx:
        kvc_ref, krc_ref, o_ref, qt_sc, vt_sc, vtc_sc = rest[:6]
    else:
        o_ref, qt_sc, vt_sc = rest[:3]
    slot0, slot1 = rest[-14:-10], rest[-10:-6]
    m_sc, l_sc, acc_sc, accx_sc, ref_sc, jump_sc = rest[-6:]
    tq = q_ref.shape[0]
    nchunks = kv_ref.shape[0] // tk
    cols = [slice(j * sub, (j + 1) * sub) for j in range(tq // sub)]

    @pl.when(pl.program_id(1) == 0)
    def _():
        def tr(c, carry):
            ks = pl.multiple_of(c * tk, tk)
            vt_sc[c, 0:MLA_V, :] = kv_ref[pl.ds(ks, tk), MLA_NOPE:MLA_NOPE + MLA_V].T
            vt_sc[c, MLA_V:MLA_VX, :] = jnp.ones((MLA_VX - MLA_V, tk), BF16)
            return carry
        lax.fori_loop(0, nchunks, tr, 0)
        if has_ctx:
            vtc_sc[0:MLA_V, :] = kvc_ref[:, MLA_NOPE:MLA_NOPE + MLA_V].T
            vtc_sc[MLA_V:MLA_VX, :] = jnp.ones((MLA_VX - MLA_V, vtc_sc.shape[1]), BF16)

    qt_sc[...] = q_ref[...].T

    def keys(c):
        ks = pl.multiple_of(c * tk, tk)
        return jnp.concatenate([kv_ref[pl.ds(ks, tk), 0:MLA_NOPE], kr_ref[pl.ds(ks, tk), :]], axis=1)

    def ctx_keys():
        return jnp.concatenate([kvc_ref[:, 0:MLA_NOPE], krc_ref[...]], axis=1)

    def scores(k):
        return [_dot(k, qt_sc[:, c]) for c in cols]

    def fast_pass():
        sts = scores(ctx_keys())
        m0 = jnp.concatenate([jnp.max(st, axis=0, keepdims=True) for st in sts], axis=1)
        vtc = vtc_sc[...]
        accx_sc[...] = jnp.concatenate(
            [_dot(vtc, jnp.exp2(st - m0[:, c]).astype(BF16)) for st, c in zip(sts, cols)], axis=1)
        m_sc[...] = m0
        ref_sc[...] = m0
        jump_sc[...] = jnp.zeros(jump_sc.shape, F32)

        def qke_stage(c, slot):
            p_sc, r_sc = slot[2], slot[3]
            m_cur = m_sc[...]
            k = keys(c)
            mxs = []
            for col in cols:
                st = _dot(k, qt_sc[:, col])
                p_sc[:, col] = jnp.exp2(st - m_cur[:, col]).astype(BF16)
                mxs.append(jnp.max(st, axis=0, keepdims=True))
            mx = jnp.concatenate(mxs, axis=1)
            r_sc[...] = m_cur
            jump_sc[...] = jnp.maximum(jump_sc[...], mx - m_cur)
            m_sc[...] = jnp.maximum(m_cur, mx)

        def pvx_stage(c, slot):
            p_sc, r_sc = slot[2], slot[3]
            vt = vt_sc[c]
            pv = jnp.concatenate([_dot(vt, p_sc[:, col]) for col in cols], axis=1)
            r = r_sc[...]
            accx_sc[...] = accx_sc[...] * jnp.exp2(ref_sc[...] - r) + pv
            ref_sc[...] = r

        qke_stage(0, slot0)

        def body(i, carry):
            qke_stage(2 * i + 1, slot1)
            pvx_stage(2 * i, slot0)
            qke_stage(2 * i + 2, slot0)
            pvx_stage(2 * i + 1, slot1)
            return carry

        lax.fori_loop(0, nchunks // 2 - 1, body, 0)
        qke_stage(nchunks - 1, slot1)
        pvx_stage(nchunks - 2, slot0)
        pvx_stage(nchunks - 1, slot1)
        accx = accx_sc[...]
        o_ref[...] = (accx[0:MLA_V] / accx[MLA_V:MLA_V + 1]).T.astype(o_ref.dtype)

    if has_ctx and nchunks > 1:
        assert nchunks % 2 == 0
        fast_pass()
        needs_exact = jnp.max(jump_sc[...]) > MLA_JUMP_LIMIT
    else:
        needs_exact = None

    @pl.when(True if needs_exact is None else needs_exact)
    def _():
        _mla_exact_pass(o_ref, kv_ref, kr_ref, kvc_ref if has_ctx else None, krc_ref if has_ctx else None,
                        qt_sc, vt_sc, vtc_sc if has_ctx else None, slot0, slot1, m_sc, l_sc, acc_sc,
                        tk=tk, cols=cols)


def _mla_exact_pass(o_ref, kv_ref, kr_ref, kvc_ref, krc_ref, qt_sc, vt_sc, vtc_sc, slot0, slot1,
                    m_sc, l_sc, acc_sc, *, tk, cols):
    has_ctx = kvc_ref is not None
    nchunks = kv_ref.shape[0] // tk
    assert nchunks == 1 or nchunks % 2 == 0
    m_sc[...] = jnp.full(m_sc.shape, NEG, F32)
    l_sc[...] = jnp.zeros(l_sc.shape, F32)
    acc_sc[...] = jnp.zeros(acc_sc.shape, F32)

    def keys(c):
        ks = pl.multiple_of(c * tk, tk)
        return jnp.concatenate([kv_ref[pl.ds(ks, tk), 0:MLA_NOPE], kr_ref[pl.ds(ks, tk), :]], axis=1)

    def scores(k):
        return [_dot(k, qt_sc[:, c]) for c in cols]

    def softmax_pv(sts, vt):
        m_prev = m_sc[...]
        m_new = jnp.maximum(m_prev, jnp.concatenate(
            [jnp.max(st, axis=0, keepdims=True) for st in sts], axis=1))
        alpha = jnp.exp2(m_prev - m_new)
        sums, pvs = [], []
        for st, c in zip(sts, cols):
            p = jnp.exp2(st - m_new[:, c])
            sums.append(jnp.sum(p, axis=0, keepdims=True))
            pvs.append(_dot(vt, p.astype(BF16)))
        l_sc[...] = alpha * l_sc[...] + jnp.concatenate(sums, axis=1)
        acc_sc[...] = alpha * acc_sc[...] + jnp.concatenate(pvs, axis=1)
        m_sc[...] = m_new

    def qk_stage(c, slot):
        st_sc, mx_sc = slot[0], slot[1]
        for col, st in zip(cols, scores(keys(c))):
            st_sc[:, col] = st
            mx_sc[:, col] = jnp.max(st, axis=0, keepdims=True)

    def softmax_stage(slot):
        st_sc, mx_sc, p_sc, al_sc = slot
        m_prev = m_sc[...]
        m_new = jnp.maximum(m_prev, mx_sc[...])
        alpha = jnp.exp2(m_prev - m_new)
        sums = []
        for col in cols:
            p = jnp.exp2(st_sc[:, col] - m_new[:, col])
            sums.append(jnp.sum(p, axis=0, keepdims=True))
            p_sc[:, col] = p.astype(BF16)
        l_sc[...] = alpha * l_sc[...] + jnp.concatenate(sums, axis=1)
        al_sc[...] = alpha
        m_sc[...] = m_new

    def pv_stage(c, slot):
        p_sc, al_sc = slot[2], slot[3]
        vt = vt_sc[c, 0:MLA_V, :]
        pv = jnp.concatenate([_dot(vt, p_sc[:, col]) for col in cols], axis=1)
        acc_sc[...] = al_sc[...] * acc_sc[...] + pv

    if has_ctx:
        kc = jnp.concatenate([kvc_ref[:, 0:MLA_NOPE], krc_ref[...]], axis=1)
        softmax_pv(scores(kc), vtc_sc[0:MLA_V, :])
    if nchunks == 1:
        softmax_pv(scores(keys(0)), vt_sc[0, 0:MLA_V, :])
    else:
        qk_stage(0, slot0)
        qk_stage(1, slot1)
        softmax_stage(slot0)

        def body(i, carry):
            softmax_stage(slot1)
            qk_stage(2 * i + 2, slot0)
            pv_stage(2 * i, slot0)
            softmax_stage(slot0)
            qk_stage(2 * i + 3, slot1)
            pv_stage(2 * i + 1, slot1)
            return carry

        lax.fori_loop(0, nchunks // 2 - 1, body, 0)
        softmax_stage(slot1)
        pv_stage(nchunks - 2, slot0)
        pv_stage(nchunks - 1, slot1)
    o_ref[...] = (acc_sc[...] / l_sc[...]).T.astype(o_ref.dtype)


def _mla_attention(q, kv, kr, kv_c=None, kr_c=None):
    nq = q.shape[0]
    s = kv.shape[0]
    heads = q.shape[1] // MLA_Q_OUT
    tq = _pick(nq, (2048, 1024, 512, 256, 128))
    sub = _pick(tq, (512, 256, 128))
    tk = _pick(s, (512, 256, 128))
    has_ctx = kv_c is not None
    kvw = MLA_NOPE + MLA_V
    in_specs = [pl.BlockSpec((tq, MLA_Q_OUT), lambda h, i: (i, h)),
                pl.BlockSpec((s, kvw), lambda h, i: (0, h)),
                pl.BlockSpec((s, LANES), lambda h, i: (0, 0))]
    args = [q, kv, kr]
    scratch = [pltpu.VMEM((MLA_Q_OUT, tq), BF16), pltpu.VMEM((s // tk, MLA_VX, tk), BF16)]
    if has_ctx:
        sc = kv_c.shape[0]
        in_specs += [pl.BlockSpec((sc, kvw), lambda h, i: (0, h)),
                     pl.BlockSpec((sc, LANES), lambda h, i: (0, 0))]
        args += [kv_c, kr_c]
        scratch.append(pltpu.VMEM((MLA_VX, sc), BF16))
    slot = [pltpu.VMEM((tk, tq), F32), pltpu.VMEM((1, tq), F32), pltpu.VMEM((tk, tq), BF16), pltpu.VMEM((1, tq), F32)]
    scratch += slot + slot
    scratch += [pltpu.VMEM((1, tq), F32), pltpu.VMEM((1, tq), F32), pltpu.VMEM((MLA_V, tq), F32),
                pltpu.VMEM((MLA_VX, tq), F32), pltpu.VMEM((1, tq), F32), pltpu.VMEM((1, tq), F32)]
    return pl.pallas_call(
        functools.partial(_mla_attn_kernel, tk=tk, sub=sub, has_ctx=has_ctx),
        out_shape=jax.ShapeDtypeStruct((nq, heads * MLA_V), BF16),
        grid=(heads, nq // tq),
        in_specs=in_specs,
        out_specs=pl.BlockSpec((tq, MLA_V), lambda h, i: (i, h)),
        scratch_shapes=scratch,
        compiler_params=_params("arbitrary", "arbitrary"),
        name="mla_attention",
    )(*args)


GLU_HALO = 16


def _glu_down_kernel(g_ref, v_ref, gp_ref, gn_ref, cw_ref, cb_ref, w_ref, o_ref):
    i = pl.program_id(0)
    k = pl.program_id(1)
    tm = g_ref.shape[0]

    @pl.when(k == 0)
    def _():
        o_ref[...] = jnp.zeros(o_ref.shape, F32)

    g = g_ref[...].astype(F32)
    prev_row = jnp.where(i > 0, gp_ref[GLU_HALO - 1:GLU_HALO, :].astype(F32), 0.0)
    next_row = jnp.where(i < pl.num_programs(0) - 1, gn_ref[0:1, :].astype(F32), 0.0)
    row = lax.broadcasted_iota(jnp.int32, (tm, 1), 0)
    g_dn = jnp.where(row == 0, prev_row, pltpu.roll(g, 1, axis=0))
    g_up = jnp.where(row == tm - 1, next_row, pltpu.roll(g, tm - 1, axis=0))
    cw = cw_ref[...]
    gc = g_dn * cw[0:1, :] + g * cw[1:2, :] + g_up * cw[2:3, :] + cb_ref[...]
    a = (gc * jax.nn.sigmoid(gc)) * v_ref[...].astype(F32)
    o_ref[...] = _dot(a.astype(BF16), w_ref[...]) + o_ref[...]


def _glu_down(u, conv_w, conv_b, w_down):
    n = u.shape[0]
    dff, d = w_down.shape
    tm = _pick(n, (512, 256, 128))
    tk = _pick(dff, (1024, 512, 256, 128))
    nk = dff // tk
    hb = tm // GLU_HALO
    nhb = n // GLU_HALO
    return pl.pallas_call(
        _glu_down_kernel,
        out_shape=jax.ShapeDtypeStruct((n, d), F32),
        grid=(n // tm, nk),
        in_specs=[pl.BlockSpec((tm, tk), lambda i, k: (i, k)),
                  pl.BlockSpec((tm, tk), lambda i, k: (i, nk + k)),
                  pl.BlockSpec((GLU_HALO, tk), lambda i, k: (jnp.maximum(i * hb - 1, 0), k)),
                  pl.BlockSpec((GLU_HALO, tk), lambda i, k: (jnp.minimum((i + 1) * hb, nhb - 1), k)),
                  pl.BlockSpec((3, tk), lambda i, k: (0, k)),
                  pl.BlockSpec((1, tk), lambda i, k: (0, k)),
                  pl.BlockSpec((tk, d), lambda i, k: (k, 0))],
        out_specs=pl.BlockSpec((tm, d), lambda i, k: (i, 0)),
        compiler_params=_params("arbitrary", "arbitrary"),
        name="glu_down",
    )(u, u, u, u, conv_w, conv_b, w_down)


def _conv_glu(h, w_up, conv_w, conv_b, w_down):
    return _glu_down(_mm(h, w_up, BF16), conv_w, conv_b, w_down)


def _ab_mixer(hl, hc, w_in, rpb, pool_w, pool_scale, w_out, need_ctx):
    n = hl.shape[0]
    naw = w_out.shape[0] // 2
    w_in = w_in.astype(BF16)
    w_out = w_out.astype(BF16)
    pool_w = pool_w.astype(BF16)
    pool_scale = pool_scale.reshape(1, -1)
    qkv_l = _mm(hl, w_in, BF16, 0, 3 * naw)
    qkv_c = _mm(hc, w_in, BF16, 0, 3 * naw)
    o_a = _na_attention(qkv_l, qkv_c, _na_bias(rpb, n // GRID_W))
    o_b = _group_mm(_pool_diff(_mm(hl, w_in, F32, 3 * naw)), pool_w, pool_scale)
    yl = _mm2(o_a, o_b, w_out, F32)
    yc = None
    if need_ctx:
        o_ac = _dense_attention(qkv_c)
        o_bc = _group_mm(_pool_diff(_mm(hc, w_in, F32, 3 * naw)), pool_w, pool_scale)
        yc = _mm2(o_ac, o_bc, w_out, F32)
    return yl, yc


def _mla_mixer(hl, hc, w_down, g_q, g_kv, w_uq, w_ukv, w_out, tabs_l, tabs_c, need_ctx):
    qr = g_q.shape[0]
    kvr = g_kv.shape[0]
    heads = w_ukv.shape[1] // (MLA_NOPE + MLA_V)
    w_down_ext = jnp.concatenate([w_down[:, :qr + kvr], _rope_cols(w_down[:, qr + kvr:])], axis=1).astype(BF16)
    w_uq_h = w_uq.reshape(qr, heads, MLA_NOPE + MLA_ROPE)
    w_uq_rope = w_uq_h[..., MLA_NOPE:]
    w_uq_ext = jnp.concatenate(
        [w_uq_h[..., :MLA_NOPE], w_uq_rope, w_uq_rope[..., _rope_partner()]],
        axis=-1).reshape(qr, heads * MLA_Q_IN).astype(BF16)
    cs_l = jnp.concatenate([tabs_l[0][:, :MLA_ROPE], tabs_l[1][:, :MLA_ROPE]], axis=1)
    cs_c = jnp.concatenate([tabs_c[0][:, :MLA_ROPE], tabs_c[1][:, :MLA_ROPE]], axis=1)
    w_ukv = w_ukv.astype(BF16)
    w_out = w_out.astype(BF16)
    g_q = g_q.reshape(1, -1)
    g_kv = g_kv.reshape(1, -1)
    qn_l, ckv_l, kr_l = _mla_down(hl, w_down_ext, g_q, g_kv, *tabs_l)
    qn_c, ckv_c, kr_c = _mla_down(hc, w_down_ext, g_q, g_kv, *tabs_c)
    q_l = _q_up(qn_l, w_uq_ext, cs_l, heads)
    kv_l = _mm(ckv_l, w_ukv, BF16)
    kv_c = _mm(ckv_c, w_ukv, BF16)
    yl = _mm(_mla_attention(q_l, kv_l, kr_l, kv_c, kr_c), w_out, F32)
    yc = None
    if need_ctx:
        q_c = _q_up(qn_c, w_uq_ext, cs_c, heads)
        yc = _mm(_mla_attention(q_c, kv_c, kr_c), w_out, F32)
    return yl, yc


def kernel(x, c, ctx, c_ctx, w_ada, b_ada, g_mix_pre, g_mix_post, g_ffn_pre, g_ffn_post, ab_w_in, na_rpb, pool_w, pool_scale, ab_w_out, mla_w_down, mla_g_q, mla_g_kv, mla_w_uq, mla_w_ukv, mla_w_out, ffn_w_up, ffn_conv_w, ffn_conv_b, ffn_w_down):
    batch, n, d = x.shape
    lc = ctx.shape[1]
    depth = w_ada.shape[0]
    assert batch == 1 and c.shape[0] == 1
    xl = x.reshape(n, d)
    xc = ctx.reshape(lc, d)

    cvec = jnp.concatenate([c, c_ctx[None, :], jnp.zeros((6, d), F32)], axis=0)
    mod = _ada(cvec, w_ada, b_ada)

    def mods(l, r):
        return [mod[l, r:r + 1, j * d:(j + 1) * d] for j in range(6)]

    tabs_l = _rope_tables(n)
    tabs_c = (jnp.concatenate([jnp.ones((lc, MLA_ROPE), F32), jnp.zeros((lc, LANES - MLA_ROPE), F32)], axis=1),
              jnp.zeros((lc, LANES), F32))

    vec = lambda a: a.reshape(1, -1)
    sh1, sc1, _, _, _, _ = mods(0, 0)
    sh1c, sc1c, _, _, _, _ = mods(0, 1)
    hl = _norm_mod(xl, vec(g_mix_pre[0]), sc1, sh1)
    hc = _norm_mod(xc, vec(g_mix_pre[0]), sc1c, sh1c)
    for l in range(depth):
        last = l == depth - 1
        sh1, sc1, gt1, sh2, sc2, gt2 = mods(l, 0)
        sh1c, sc1c, gt1c, sh2c, sc2c, gt2c = mods(l, 1)
        if l % 2 == 0:
            e = l // 2
            yl, yc = _ab_mixer(hl, hc, ab_w_in[e], na_rpb[e], pool_w[e], pool_scale[e], ab_w_out[e], not last)
        else:
            o = l // 2
            yl, yc = _mla_mixer(hl, hc, mla_w_down[o], mla_g_q[o], mla_g_kv[o], mla_w_uq[o], mla_w_ukv[o],
                                mla_w_out[o], tabs_l, tabs_c, not last)
        w_up = ffn_w_up[l].astype(BF16)
        w_dn = ffn_w_down[l].astype(BF16)
        conv_b = vec(ffn_conv_b[l])
        xl, h2 = _resid(xl, yl, vec(g_mix_post[l]), gt1, (vec(g_ffn_pre[l]), sc2, sh2))
        f = _conv_glu(h2, w_up, ffn_conv_w[l], conv_b, w_dn)
        if last:
            xl, _ = _resid(xl, f, vec(g_ffn_post[l]), gt2)
        else:
            nsh1, nsc1, _, _, _, _ = mods(l + 1, 0)
            xl, hl = _resid(xl, f, vec(g_ffn_post[l]), gt2, (vec(g_mix_pre[l + 1]), nsc1, nsh1))
            xc, h2c = _resid(xc, yc, vec(g_mix_post[l]), gt1c, (vec(g_ffn_pre[l]), sc2c, sh2c))
            fc = _conv_glu(h2c, w_up, ffn_conv_w[l], conv_b, w_dn)
            nsh1c, nsc1c, _, _, _, _ = mods(l + 1, 1)
            xc, hc = _resid(xc, fc, vec(g_ffn_post[l]), gt2c, (vec(g_mix_pre[l + 1]), nsc1c, nsh1c))
    return xl.reshape(batch, n, d)
```

```python
import functools

import numpy as np
import jax
import jax.numpy as jnp
from jax import lax
from jax.experimental import pallas as pl
from jax.experimental.pallas import tpu as pltpu

GRID_W = 64
EPS = 1e-6
NA_HEAD_DIM = 128
NA_WIN_R = 8
NA_WIN_C = 16
NA_Q_ROWS = 4
NA_K_ROWS = NA_Q_ROWS + NA_WIN_R
POOL_WINDOWS = (2, 4, 8, 16)
POOL_HALO = 8
MLA_NOPE = 128
MLA_ROPE = 64
MLA_V = 128
ROPE_THETA = 10000.0
LANES = 128
NEG = -1e30
LOG2E = 1.4426950408889634
VMEM_LIMIT = 56 * 1024 * 1024

F32 = jnp.float32
BF16 = jnp.bfloat16


def _pick(n, prefs):
    for p in prefs:
        if n % p == 0:
            return p
    return n


def _params(*sem):
    return pltpu.CompilerParams(dimension_semantics=sem, vmem_limit_bytes=VMEM_LIMIT)


def _rms(x):
    return x * lax.rsqrt(jnp.mean(x * x, axis=-1, keepdims=True) + EPS)


def _dot(a, b):
    return jnp.dot(a, b, preferred_element_type=F32)


def _dot_nt(a, b):
    return lax.dot_general(a, b, (((1,), (1,)), ((), ())), preferred_element_type=F32)


def _ada_kernel(c_ref, w_ref, b_ref, o_ref):
    c = c_ref[...]
    s = (c * jax.nn.sigmoid(c)).astype(BF16)
    o_ref[...] = _dot(s, w_ref[...].astype(BF16)) + b_ref[...]


def _ada(cvec, w_ada, b_ada):
    depth, d, n6 = w_ada.shape
    rows = cvec.shape[0]
    tn = _pick(n6, (512, 256, 128))
    return pl.pallas_call(
        _ada_kernel,
        out_shape=jax.ShapeDtypeStruct((depth, rows, n6), F32),
        grid=(depth, n6 // tn),
        in_specs=[
            pl.BlockSpec((rows, d), lambda l, j: (0, 0)),
            pl.BlockSpec((None, d, tn), lambda l, j: (l, 0, j)),
            pl.BlockSpec((None, 1, tn), lambda l, j: (l, 0, j)),
        ],
        out_specs=pl.BlockSpec((None, rows, tn), lambda l, j: (l, 0, j)),
        compiler_params=_params("arbitrary", "arbitrary"),
        name="ada",
    )(cvec, w_ada, b_ada.reshape(depth, 1, n6))


def _norm_mod_kernel(x_ref, g_ref, sc_ref, sh_ref, o_ref):
    y = _rms(x_ref[...]) * g_ref[...]
    o_ref[...] = (y * (1.0 + sc_ref[...]) + sh_ref[...]).astype(o_ref.dtype)


def _norm_mod(x, g, sc, sh):
    m, d = x.shape
    tm = _pick(m, (256, 128, 64, 32, 16))
    vec = pl.BlockSpec((1, d), lambda i: (0, 0))
    return pl.pallas_call(
        _norm_mod_kernel,
        out_shape=jax.ShapeDtypeStruct((m, d), BF16),
        grid=(m // tm,),
        in_specs=[pl.BlockSpec((tm, d), lambda i: (i, 0)), vec, vec, vec],
        out_specs=pl.BlockSpec((tm, d), lambda i: (i, 0)),
        compiler_params=_params("arbitrary"),
        name="norm_mod",
    )(x, g, sc, sh)


def _resid_kernel(x_ref, y_ref, gpost_ref, gt_ref, *rest, with_h):
    xn = x_ref[...] + gt_ref[...] * (_rms(y_ref[...]) * gpost_ref[...])
    if with_h:
        gpre_ref, sc_ref, sh_ref, xo_ref, ho_ref = rest
        xo_ref[...] = xn
        h = _rms(xn) * gpre_ref[...]
        ho_ref[...] = (h * (1.0 + sc_ref[...]) + sh_ref[...]).astype(ho_ref.dtype)
    else:
        (xo_ref,) = rest
        xo_ref[...] = xn


def _resid(x, y, gpost, gt, pre=None):
    m, d = x.shape
    tm = _pick(m, (256, 128, 64, 32, 16))
    vec = pl.BlockSpec((1, d), lambda i: (0, 0))
    row = pl.BlockSpec((tm, d), lambda i: (i, 0))
    with_h = pre is not None
    out_shape = [jax.ShapeDtypeStruct((m, d), F32)]
    out_specs = [row]
    args = [x, y, gpost, gt]
    in_specs = [row, row, vec, vec]
    if with_h:
        out_shape.append(jax.ShapeDtypeStruct((m, d), BF16))
        out_specs.append(row)
        args += list(pre)
        in_specs += [vec, vec, vec]
    out = pl.pallas_call(
        functools.partial(_resid_kernel, with_h=with_h),
        out_shape=out_shape,
        grid=(m // tm,),
        in_specs=in_specs,
        out_specs=out_specs,
        compiler_params=_params("arbitrary"),
        name="resid",
    )(*args)
    return (out[0], out[1]) if with_h else (out[0], None)


def _mm_kernel(a_ref, w_ref, o_ref):
    o_ref[...] = _dot(a_ref[...], w_ref[...]).astype(o_ref.dtype)


def _mm(a, w, out_dtype, col0=0, ncols=None):
    m, k = a.shape
    n = w.shape[1] - col0 if ncols is None else ncols
    tm = _pick(m, (1024, 1280, 512, 256, 128))
    tn = _pick(n, (1024, 512, 256, 128))
    assert col0 % tn == 0
    jb = col0 // tn
    return pl.pallas_call(
        _mm_kernel,
        out_shape=jax.ShapeDtypeStruct((m, n), out_dtype),
        grid=(m // tm, n // tn),
        in_specs=[pl.BlockSpec((tm, k), lambda i, j: (i, 0)),
                  pl.BlockSpec((k, tn), lambda i, j: (0, jb + j))],
        out_specs=pl.BlockSpec((tm, tn), lambda i, j: (i, j)),
        compiler_params=_params("arbitrary", "arbitrary"),
        name="mm",
    )(a, w)


def _mm2_kernel(a1_ref, w1_ref, a2_ref, w2_ref, o_ref):
    o_ref[...] = (_dot(a1_ref[...], w1_ref[...]) + _dot(a2_ref[...], w2_ref[...])).astype(o_ref.dtype)


def _mm2(a1, a2, w, out_dtype):
    m, k1 = a1.shape
    k2 = a2.shape[1]
    n = w.shape[1]
    assert k1 == k2 and w.shape[0] == k1 + k2
    tm = _pick(m, (1024, 512, 256, 128))
    tn = _pick(n, (1024, 512, 256, 128))
    return pl.pallas_call(
        _mm2_kernel,
        out_shape=jax.ShapeDtypeStruct((m, n), out_dtype),
        grid=(m // tm, n // tn),
        in_specs=[pl.BlockSpec((tm, k1), lambda i, j: (i, 0)),
                  pl.BlockSpec((k1, tn), lambda i, j: (0, j)),
                  pl.BlockSpec((tm, k2), lambda i, j: (i, 0)),
                  pl.BlockSpec((k2, tn), lambda i, j: (1, j))],
        out_specs=pl.BlockSpec((tm, tn), lambda i, j: (i, j)),
        compiler_params=_params("arbitrary", "arbitrary"),
        name="mm2",
    )(a1, w, a2, w)


def _na_bias(rpb, rows):
    w = GRID_W
    h = rpb.shape[0]
    qc = np.arange(w)[:, None]
    kc = np.arange(w)[None, :]
    c0 = np.clip(qc - NA_WIN_C // 2, 0, w - NA_WIN_C)
    ok_c = (kc >= c0) & (kc < c0 + NA_WIN_C)
    dc = np.clip(kc - qc + NA_WIN_C - 1, 0, 2 * NA_WIN_C - 2)
    onehot = (dc[None] == np.arange(2 * NA_WIN_C - 1)[:, None, None]).astype(np.float32)
    t = jnp.einsum('hdj,jqk->hdqk', rpb.astype(F32), onehot, precision=lax.Precision.HIGHEST)
    t = jnp.where(ok_c, t, NEG)
    masked = jnp.full((h, w, w), NEG, F32)
    out = []
    for r_first, k_first in ((0, 0), (NA_Q_ROWS, 0), (rows - NA_Q_ROWS, rows - NA_K_ROWS)):
        q_rows = []
        for qr in range(NA_Q_ROWS):
            r = r_first + qr
            r0 = min(max(r - NA_WIN_R // 2, 0), rows - NA_WIN_R)
            blocks = []
            for kr in range(NA_K_ROWS):
                ka = k_first + kr
                blocks.append(t[:, ka - r + NA_WIN_R - 1] if r0 <= ka < r0 + NA_WIN_R else masked)
            q_rows.append(jnp.concatenate(blocks, axis=2))
        out.append(jnp.concatenate(q_rows, axis=1))
    return jnp.stack(out) * LOG2E


def _na_kernel(q_ref, k0_ref, k1_ref, k2_ref, v0_ref, v1_ref, v2_ref, kc_ref, vc_ref, b_ref, o_ref,
               *, heads):
    qb = q_ref.shape[0]
    hd = [slice(h * NA_HEAD_DIM, (h + 1) * NA_HEAD_DIM) for h in range(heads)]
    qs = [q_ref[:, c] for c in hd]
    s_loc = [jnp.concatenate([_dot_nt(q, k_ref[:, c]) for k_ref in (k0_ref, k1_ref, k2_ref)], axis=1)
             + b_ref[h] for h, (q, c) in enumerate(zip(qs, hd))]
    s_ctx = [_dot_nt(q, kc_ref[:, c]) for q, c in zip(qs, hd)]
    ms = [jnp.maximum(jnp.max(sl, axis=1, keepdims=True), jnp.max(sc, axis=1, keepdims=True))
          for sl, sc in zip(s_loc, s_ctx)]
    p_loc = [jnp.exp2(sl - m) for sl, m in zip(s_loc, ms)]
    p_ctx = [jnp.exp2(sc - m) for sc, m in zip(s_ctx, ms)]
    for h, c in enumerate(hd):
        denom = jnp.sum(p_loc[h], axis=1, keepdims=True) + jnp.sum(p_ctx[h], axis=1, keepdims=True)
        o = _dot(p_ctx[h].astype(BF16), vc_ref[:, c])
        for j, v_ref in enumerate((v0_ref, v1_ref, v2_ref)):
            o = o + _dot(p_loc[h][:, j * qb:(j + 1) * qb].astype(BF16), v_ref[:, c])
        o_ref[:, c] = (o / denom).astype(o_ref.dtype)


def _na_attention(qkv_l, qkv_c, bias):
    n = qkv_l.shape[0]
    lc = qkv_c.shape[0]
    width = qkv_l.shape[1] // 3
    heads = width // NA_HEAD_DIM
    qb = NA_Q_ROWS * GRID_W
    nblk = n // qb
    assert n % qb == 0 and nblk >= 3
    hb = _pick(heads, (2, 1))
    hg = heads // hb
    bw = hb * NA_HEAD_DIM

    def kv_spec(j, col0):
        return pl.BlockSpec((qb, bw), lambda h, i: (jnp.clip(i - 1, 0, nblk - 3) + j, col0 + h))

    def bias_map(h, i):
        return (jnp.where(i == 0, 0, jnp.where(i == nblk - 1, 2, 1)), h, 0, 0)

    return pl.pallas_call(
        functools.partial(_na_kernel, heads=hb),
        out_shape=jax.ShapeDtypeStruct((n, width), BF16),
        grid=(hg, nblk),
        in_specs=[
            pl.BlockSpec((qb, bw), lambda h, i: (i, h)),
            kv_spec(0, hg), kv_spec(1, hg), kv_spec(2, hg),
            kv_spec(0, 2 * hg), kv_spec(1, 2 * hg), kv_spec(2, 2 * hg),
            pl.BlockSpec((lc, bw), lambda h, i: (0, hg + h)),
            pl.BlockSpec((lc, bw), lambda h, i: (0, 2 * hg + h)),
            pl.BlockSpec((None, hb, qb, 3 * qb), bias_map),
        ],
        out_specs=pl.BlockSpec((qb, bw), lambda h, i: (i, h)),
        compiler_params=_params("arbitrary", "arbitrary"),
        name="na_attention",
    )(qkv_l, qkv_l, qkv_l, qkv_l, qkv_l, qkv_l, qkv_l, qkv_c, qkv_c, bias)


def _dense_attn_kernel(q_ref, k_ref, v_ref, o_ref):
    s = _dot_nt(q_ref[...], k_ref[...])
    p = jnp.exp2(s - jnp.max(s, axis=1, keepdims=True))
    o = _dot(p.astype(BF16), v_ref[...])
    o_ref[...] = (o / jnp.sum(p, axis=1, keepdims=True)).astype(o_ref.dtype)


def _dense_attention(qkv):
    n = qkv.shape[0]
    width = qkv.shape[1] // 3
    heads = width // NA_HEAD_DIM
    return pl.pallas_call(
        _dense_attn_kernel,
        out_shape=jax.ShapeDtypeStruct((n, width), BF16),
        grid=(heads,),
        in_specs=[pl.BlockSpec((n, NA_HEAD_DIM), lambda h: (0, h)),
                  pl.BlockSpec((n, NA_HEAD_DIM), lambda h: (0, heads + h)),
                  pl.BlockSpec((n, NA_HEAD_DIM), lambda h: (0, 2 * heads + h))],
        out_specs=pl.BlockSpec((n, NA_HEAD_DIM), lambda h: (0, h)),
        compiler_params=_params("arbitrary"),
        name="dense_attention",
    )(qkv, qkv, qkv)


def _pool_kernel(x_ref, o_ref, u_ref, *, n, chunk, blocks_per_group):
    group = pl.program_id(0) // blocks_per_group
    rows = chunk + 2 * POOL_HALO
    u_ref[0:POOL_HALO, :] = jnp.zeros((POOL_HALO, LANES), F32)
    u_ref[n + POOL_HALO:n + 2 * POOL_HALO, :] = jnp.zeros((POOL_HALO, LANES), F32)

    def fill(c, carry):
        base = pl.multiple_of(c * chunk, chunk)
        u_ref[pl.ds(base + POOL_HALO, chunk), :] = x_ref[pl.ds(base, chunk), :]
        return carry
    lax.fori_loop(0, n // chunk, fill, 0)

    def window_sum(xs, w):
        a = pltpu.roll(xs, 1, axis=0) + xs
        half = 1
        while 2 * half < w:
            a = pltpu.roll(a, half, axis=0) + pltpu.roll(a, rows - half, axis=0)
            half *= 2
        return a

    for gi, w in enumerate(POOL_WINDOWS):
        @pl.when(group == gi)
        def _(w=w):
            def body(c, carry):
                base = pl.multiple_of(c * chunk, chunk)
                xs = u_ref[pl.ds(base, rows), :]
                tot = window_sum(xs, w)[POOL_HALO:POOL_HALO + chunk]
                t = base + lax.broadcasted_iota(jnp.int32, (chunk, 1), 0)
                cnt = jnp.minimum(t + w // 2, n) - jnp.maximum(t - w // 2, 0)
                mean = tot / cnt.astype(F32)
                o_ref[pl.ds(base, chunk), :] = (mean - xs[POOL_HALO:POOL_HALO + chunk]).astype(o_ref.dtype)
                return carry
            lax.fori_loop(0, n // chunk, body, 0)


def _pool_diff(u):
    n, c = u.shape
    cg = c // len(POOL_WINDOWS)
    assert cg % LANES == 0
    chunk = _pick(n, (512, 256, 128))
    return pl.pallas_call(
        functools.partial(_pool_kernel, n=n, chunk=chunk, blocks_per_group=cg // LANES),
        out_shape=jax.ShapeDtypeStruct((n, c), BF16),
        grid=(c // LANES,),
        in_specs=[pl.BlockSpec((n, LANES), lambda j: (0, j))],
        out_specs=pl.BlockSpec((n, LANES), lambda j: (0, j)),
        scratch_shapes=[pltpu.VMEM((n + 2 * POOL_HALO, LANES), F32)],
        compiler_params=_params("arbitrary"),
        name="pool_diff",
    )(u)


def _group_mm_kernel(a_ref, w_ref, s_ref, o_ref):
    o_ref[...] = (_dot(a_ref[...], w_ref[...]) * s_ref[...]).astype(o_ref.dtype)


def _group_mm(d, w_grp, ch_scale):
    n, c = d.shape
    g, cg, _ = w_grp.shape
    tm = _pick(n, (1024, 512, 256, 128))
    return pl.pallas_call(
        _group_mm_kernel,
        out_shape=jax.ShapeDtypeStruct((n, c), BF16),
        grid=(g, n // tm),
        in_specs=[pl.BlockSpec((tm, cg), lambda gi, i: (i, gi)),
                  pl.BlockSpec((None, cg, cg), lambda gi, i: (gi, 0, 0)),
                  pl.BlockSpec((1, cg), lambda gi, i: (0, gi))],
        out_specs=pl.BlockSpec((tm, cg), lambda gi, i: (i, gi)),
        compiler_params=_params("arbitrary", "arbitrary"),
        name="group_mm",
    )(d, w_grp, ch_scale)


def _rope_tables(n):
    nf = MLA_ROPE // 4
    t = jnp.arange(n)
    row = (t // GRID_W).astype(F32)
    col = (t % GRID_W).astype(F32)
    freqs = ROPE_THETA ** (-jnp.arange(nf, dtype=F32) / nf)
    ang = jnp.stack([row[:, None] * freqs, col[:, None] * freqs], axis=1)
    cos, sin = jnp.cos(ang), jnp.sin(ang)
    c = jnp.stack([cos, cos], axis=2).reshape(n, MLA_ROPE)
    s = jnp.stack([-sin, sin], axis=2).reshape(n, MLA_ROPE)
    pad = jnp.zeros((n, LANES - MLA_ROPE), F32)
    return jnp.concatenate([c, pad], axis=1), jnp.concatenate([s, pad], axis=1)


def _rope_partner():
    nf = MLA_ROPE // 4
    j = np.arange(MLA_ROPE)
    return (j // (2 * nf)) * (2 * nf) + (1 - (j // nf) % 2) * nf + j % nf


def _rope_cols(w):
    z = jnp.zeros((w.shape[0], LANES - MLA_ROPE), w.dtype)
    return jnp.concatenate([w, z, w[:, _rope_partner()], z], axis=1)


def _mla_down_kernel(h_ref, w_ref, gq_ref, gkv_ref, c_ref, s_ref, qn_ref, ckv_ref, kr_ref, *, qr, kvr):
    z = _dot(h_ref[...], w_ref[...])
    qn_ref[...] = (_rms(z[:, :qr]) * gq_ref[...]).astype(qn_ref.dtype)
    ckv_ref[...] = (_rms(z[:, qr:qr + kvr]) * gkv_ref[...]).astype(ckv_ref.dtype)
    r0 = qr + kvr
    rot = z[:, r0:r0 + LANES] * c_ref[...] + z[:, r0 + LANES:r0 + 2 * LANES] * s_ref[...]
    kr_ref[...] = rot.astype(kr_ref.dtype)


def _mla_down(h, w_ext, g_q, g_kv, ctab, stab):
    n, d = h.shape
    qr = g_q.shape[1]
    kvr = g_kv.shape[1]
    nw = w_ext.shape[1]
    tm = _pick(n, (512, 256, 128))
    return pl.pallas_call(
        functools.partial(_mla_down_kernel, qr=qr, kvr=kvr),
        out_shape=[jax.ShapeDtypeStruct((n, qr), BF16),
                   jax.ShapeDtypeStruct((n, kvr), BF16),
                   jax.ShapeDtypeStruct((n, LANES), BF16)],
        grid=(n // tm,),
        in_specs=[pl.BlockSpec((tm, d), lambda i: (i, 0)),
                  pl.BlockSpec((d, nw), lambda i: (0, 0)),
                  pl.BlockSpec((1, qr), lambda i: (0, 0)),
                  pl.BlockSpec((1, kvr), lambda i: (0, 0)),
                  pl.BlockSpec((tm, LANES), lambda i: (i, 0)),
                  pl.BlockSpec((tm, LANES), lambda i: (i, 0))],
        out_specs=[pl.BlockSpec((tm, qr), lambda i: (i, 0)),
                   pl.BlockSpec((tm, kvr), lambda i: (i, 0)),
                   pl.BlockSpec((tm, LANES), lambda i: (i, 0))],
        compiler_params=_params("arbitrary"),
        name="mla_down",
    )(h, w_ext, g_q, g_kv, ctab, stab)


MLA_Q_IN = MLA_NOPE + LANES
MLA_Q_OUT = MLA_NOPE + LANES
MLA_VX = MLA_V + 16
MLA_JUMP_LIMIT = 64.0


def _q_up_kernel(a_ref, w_ref, cs_ref, o_ref, *, heads, scale):
    z = _dot(a_ref[...], w_ref[...])
    cs = cs_ref[...] * scale
    for j in range(heads):
        zi = j * MLA_Q_IN
        oi = j * MLA_Q_OUT
        o_ref[:, oi:oi + MLA_NOPE] = (z[:, zi:zi + MLA_NOPE] * scale).astype(o_ref.dtype)
        t = z[:, zi + MLA_NOPE:zi + MLA_Q_IN] * cs
        rot = t + pltpu.roll(t, LANES // 2, axis=1)
        o_ref[:, oi + MLA_NOPE:oi + MLA_Q_OUT] = rot.astype(o_ref.dtype)


def _q_up(qn, w_ext, cstab, heads):
    n, qr = qn.shape
    tm = _pick(n, (1024, 512, 256, 128))
    hb = _pick(heads, (4, 2, 1))
    scale = (MLA_NOPE + MLA_ROPE) ** -0.5 * LOG2E
    return pl.pallas_call(
        functools.partial(_q_up_kernel, heads=hb, scale=scale),
        out_shape=jax.ShapeDtypeStruct((n, heads * MLA_Q_OUT), BF16),
        grid=(n // tm, heads // hb),
        in_specs=[pl.BlockSpec((tm, qr), lambda i, j: (i, 0)),
                  pl.BlockSpec((qr, hb * MLA_Q_IN), lambda i, j: (0, j)),
                  pl.BlockSpec((tm, LANES), lambda i, j: (i, 0))],
        out_specs=pl.BlockSpec((tm, hb * MLA_Q_OUT), lambda i, j: (i, j)),
        compiler_params=_params("arbitrary", "arbitrary"),
        name="mla_q_up",
    )(qn, w_ext, cstab)


def _mla_attn_kernel(q_ref, kv_ref, kr_ref, *rest, tk, sub, has_ctx):
    if has_ctx:
        kvc_ref, krc_ref, o_ref, qt_sc, vt_sc, vtc_sc = rest[:6]
    else:
        o_ref, qt_sc, vt_sc = rest[:3]
    slot0, slot1 = rest[-14:-10], rest[-10:-6]
    m_sc, l_sc, acc_sc, accx_sc, ref_sc, jump_sc = rest[-6:]
    tq = q_ref.shape[0]
    nchunks = kv_ref.shape[0] // tk
    cols = [slice(j * sub, (j + 1) * sub) for j in range(tq // sub)]

    @pl.when(pl.program_id(1) == 0)
    def _():
        def tr(c, carry):
            ks = pl.multiple_of(c * tk, tk)
            vt_sc[c, 0:MLA_V, :] = kv_ref[pl.ds(ks, tk), MLA_NOPE:MLA_NOPE + MLA_V].T
            vt_sc[c, MLA_V:MLA_VX, :] = jnp.ones((MLA_VX - MLA_V, tk), BF16)
            return carry
        lax.fori_loop(0, nchunks, tr, 0)
        if has_ctx:
            vtc_sc[0:MLA_V, :] = kvc_ref[:, MLA_NOPE:MLA_NOPE + MLA_V].T
            vtc_sc[MLA_V:MLA_VX, :] = jnp.ones((MLA_VX - MLA_V, vtc_sc.shape[1]), BF16)

    qt_sc[...] = q_ref[...].T

    def keys(c):
        ks = pl.multiple_of(c * tk, tk)
        return jnp.concatenate([kv_ref[pl.ds(ks, tk), 0:MLA_NOPE], kr_ref[pl.ds(ks, tk), :]], axis=1)

    def ctx_keys():
        return jnp.concatenate([kvc_ref[:, 0:MLA_NOPE], krc_ref[...]], axis=1)

    def scores(k):
        return [_dot(k, qt_sc[:, c]) for c in cols]

    def fast_pass():
        sts = scores(ctx_keys())
        m0 = jnp.concatenate([jnp.max(st, axis=0, keepdims=True) for st in sts], axis=1)
        vtc = vtc_sc[...]
        accx_sc[...] = jnp.concatenate(
            [_dot(vtc, jnp.exp2(st - m0[:, c]).astype(BF16)) for st, c in zip(sts, cols)], axis=1)
        m_sc[...] = m0
        ref_sc[...] = m0
        jump_sc[...] = jnp.zeros(jump_sc.shape, F32)

        def qke_stage(c, slot):
            p_sc, r_sc = slot[2], slot[3]
            m_cur = m_sc[...]
            k = keys(c)
            mxs = []
            for col in cols:
                st = _dot(k, qt_sc[:, col])
                p_sc[:, col] = jnp.exp2(st - m_cur[:, col]).astype(BF16)
                mxs.append(jnp.max(st, axis=0, keepdims=True))
            mx = jnp.concatenate(mxs, axis=1)
            r_sc[...] = m_cur
            jump_sc[...] = jnp.maximum(jump_sc[...], mx - m_cur)
            m_sc[...] = jnp.maximum(m_cur, mx)

        def pvx_stage(c, slot):
            p_sc, r_sc = slot[2], slot[3]
            vt = vt_sc[c]
            pv = jnp.concatenate([_dot(vt, p_sc[:, col]) for col in cols], axis=1)
            r = r_sc[...]
            accx_sc[...] = accx_sc[...] * jnp.exp2(ref_sc[...] - r) + pv
            ref_sc[...] = r

        qke_stage(0, slot0)

        def body(i, carry):
            qke_stage(2 * i + 1, slot1)
            pvx_stage(2 * i, slot0)
            qke_stage(2 * i + 2, slot0)
            pvx_stage(2 * i + 1, slot1)
            return carry

        lax.fori_loop(0, nchunks // 2 - 1, body, 0)
        qke_stage(nchunks - 1, slot1)
        pvx_stage(nchunks - 2, slot0)
        pvx_stage(nchunks - 1, slot1)
        accx = accx_sc[...]
        o_ref[...] = (accx[0:MLA_V] / accx[MLA_V:MLA_V + 1]).T.astype(o_ref.dtype)

    if has_ctx and nchunks > 1:
        assert nchunks % 2 == 0
        fast_pass()
        needs_exact = jnp.max(jump_sc[...]) > MLA_JUMP_LIMIT
    else:
        needs_exact = None

    @pl.when(True if needs_exact is None else needs_exact)
    def _():
        _mla_exact_pass(o_ref, kv_ref, kr_ref, kvc_ref if has_ctx else None, krc_ref if has_ctx else None,
                        qt_sc, vt_sc, vtc_sc if has_ctx else None, slot0, slot1, m_sc, l_sc, acc_sc,
                        tk=tk, cols=cols)


def _mla_exact_pass(o_ref, kv_ref, kr_ref, kvc_ref, krc_ref, qt_sc, vt_sc, vtc_sc, slot0, slot1,
                    m_sc, l_sc, acc_sc, *, tk, cols):
    has_ctx = kvc_ref is not None
    nchunks = kv_ref.shape[0] // tk
    assert nchunks == 1 or nchunks % 2 == 0
    m_sc[...] = jnp.full(m_sc.shape, NEG, F32)
    l_sc[...] = jnp.zeros(l_sc.shape, F32)
    acc_sc[...] = jnp.zeros(acc_sc.shape, F32)

    def keys(c):
        ks = pl.multiple_of(c * tk, tk)
        return jnp.concatenate([kv_ref[pl.ds(ks, tk), 0:MLA_NOPE], kr_ref[pl.ds(ks, tk), :]], axis=1)

    def scores(k):
        return [_dot(k, qt_sc[:, c]) for c in cols]

    def softmax_pv(sts, vt):
        m_prev = m_sc[...]
        m_new = jnp.maximum(m_prev, jnp.concatenate(
            [jnp.max(st, axis=0, keepdims=True) for st in sts], axis=1))
        alpha = jnp.exp2(m_prev - m_new)
        sums, pvs = [], []
        for st, c in zip(sts, cols):
            p = jnp.exp2(st - m_new[:, c])
            sums.append(jnp.sum(p, axis=0, keepdims=True))
            pvs.append(_dot(vt, p.astype(BF16)))
        l_sc[...] = alpha * l_sc[...] + jnp.concatenate(sums, axis=1)
        acc_sc[...] = alpha * acc_sc[...] + jnp.concatenate(pvs, axis=1)
        m_sc[...] = m_new

    def qk_stage(c, slot):
        st_sc, mx_sc = slot[0], slot[1]
        for col, st in zip(cols, scores(keys(c))):
            st_sc[:, col] = st
            mx_sc[:, col] = jnp.max(st, axis=0, keepdims=True)

    def softmax_stage(slot):
        st_sc, mx_sc, p_sc, al_sc = slot
        m_prev = m_sc[...]
        m_new = jnp.maximum(m_prev, mx_sc[...])
        alpha = jnp.exp2(m_prev - m_new)
        sums = []
        for col in cols:
            p = jnp.exp2(st_sc[:, col] - m_new[:, col])
            sums.append(jnp.sum(p, axis=0, keepdims=True))
            p_sc[:, col] = p.astype(BF16)
        l_sc[...] = alpha * l_sc[...] + jnp.concatenate(sums, axis=1)
        al_sc[...] = alpha
        m_sc[...] = m_new

    def pv_stage(c, slot):
        p_sc, al_sc = slot[2], slot[3]
        vt = vt_sc[c, 0:MLA_V, :]
        pv = jnp.concatenate([_dot(vt, p_sc[:, col]) for col in cols], axis=1)
        acc_sc[...] = al_sc[...] * acc_sc[...] + pv

    if has_ctx:
        kc = jnp.concatenate([kvc_ref[:, 0:MLA_NOPE], krc_ref[...]], axis=1)
        softmax_pv(scores(kc), vtc_sc[0:MLA_V, :])
    if nchunks == 1:
        softmax_pv(scores(keys(0)), vt_sc[0, 0:MLA_V, :])
    else:
        qk_stage(0, slot0)
        qk_stage(1, slot1)
        softmax_stage(slot0)

        def body(i, carry):
            softmax_stage(slot1)
            qk_stage(2 * i + 2, slot0)
            pv_stage(2 * i, slot0)
            softmax_stage(slot0)
            qk_stage(2 * i + 3, slot1)
            pv_stage(2 * i + 1, slot1)
            return carry

        lax.fori_loop(0, nchunks // 2 - 1, body, 0)
        softmax_stage(slot1)
        pv_stage(nchunks - 2, slot0)
        pv_stage(nchunks - 1, slot1)
    o_ref[...] = (acc_sc[...] / l_sc[...]).T.astype(o_ref.dtype)


def _mla_attention(q, kv, kr, kv_c=None, kr_c=None):
    nq = q.shape[0]
    s = kv.shape[0]
    heads = q.shape[1] // MLA_Q_OUT
    tq = _pick(nq, (2048, 1024, 512, 256, 128))
    sub = _pick(tq, (512, 256, 128))
    tk = _pick(s, (512, 256, 128))
    has_ctx = kv_c is not None
    kvw = MLA_NOPE + MLA_V
    in_specs = [pl.BlockSpec((tq, MLA_Q_OUT), lambda h, i: (i, h)),
                pl.BlockSpec((s, kvw), lambda h, i: (0, h)),
                pl.BlockSpec((s, LANES), lambda h, i: (0, 0))]
    args = [q, kv, kr]
    scratch = [pltpu.VMEM((MLA_Q_OUT, tq), BF16), pltpu.VMEM((s // tk, MLA_VX, tk), BF16)]
    if has_ctx:
        sc = kv_c.shape[0]
        in_specs += [pl.BlockSpec((sc, kvw), lambda h, i: (0, h)),
                     pl.BlockSpec((sc, LANES), lambda h, i: (0, 0))]
        args += [kv_c, kr_c]
        scratch.append(pltpu.VMEM((MLA_VX, sc), BF16))
    slot = [pltpu.VMEM((tk, tq), F32), pltpu.VMEM((1, tq), F32), pltpu.VMEM((tk, tq), BF16), pltpu.VMEM((1, tq), F32)]
    scratch += slot + slot
    scratch += [pltpu.VMEM((1, tq), F32), pltpu.VMEM((1, tq), F32), pltpu.VMEM((MLA_V, tq), F32),
                pltpu.VMEM((MLA_VX, tq), F32), pltpu.VMEM((1, tq), F32), pltpu.VMEM((1, tq), F32)]
    return pl.pallas_call(
        functools.partial(_mla_attn_kernel, tk=tk, sub=sub, has_ctx=has_ctx),
        out_shape=jax.ShapeDtypeStruct((nq, heads * MLA_V), BF16),
        grid=(heads, nq // tq),
        in_specs=in_specs,
        out_specs=pl.BlockSpec((tq, MLA_V), lambda h, i: (i, h)),
        scratch_shapes=scratch,
        compiler_params=_params("arbitrary", "arbitrary"),
        name="mla_attention",
    )(*args)


GLU_HALO = 16


def _glu_down_kernel(g_ref, v_ref, gp_ref, gn_ref, cw_ref, cb_ref, w_ref, o_ref):
    i = pl.program_id(0)
    k = pl.program_id(1)
    tm = g_ref.shape[0]

    @pl.when(k == 0)
    def _():
        o_ref[...] = jnp.zeros(o_ref.shape, F32)

    g = g_ref[...].astype(F32)
    prev_row = jnp.where(i > 0, gp_ref[GLU_HALO - 1:GLU_HALO, :].astype(F32), 0.0)
    next_row = jnp.where(i < pl.num_programs(0) - 1, gn_ref[0:1, :].astype(F32), 0.0)
    row = lax.broadcasted_iota(jnp.int32, (tm, 1), 0)
    g_dn = jnp.where(row == 0, prev_row, pltpu.roll(g, 1, axis=0))
    g_up = jnp.where(row == tm - 1, next_row, pltpu.roll(g, tm - 1, axis=0))
    cw = cw_ref[...]
    gc = g_dn * cw[0:1, :] + g * cw[1:2, :] + g_up * cw[2:3, :] + cb_ref[...]
    a = (gc * jax.nn.sigmoid(gc)) * v_ref[...].astype(F32)
    o_ref[...] = _dot(a.astype(BF16), w_ref[...]) + o_ref[...]


def _glu_down(u, conv_w, conv_b, w_down):
    n = u.shape[0]
    dff, d = w_down.shape
    tm = _pick(n, (512, 256, 128))
    tk = _pick(dff, (1024, 512, 256, 128))
    nk = dff // tk
    hb = tm // GLU_HALO
    nhb = n // GLU_HALO
    return pl.pallas_call(
        _glu_down_kernel,
        out_shape=jax.ShapeDtypeStruct((n, d), F32),
        grid=(n // tm, nk),
        in_specs=[pl.BlockSpec((tm, tk), lambda i, k: (i, k)),
                  pl.BlockSpec((tm, tk), lambda i, k: (i, nk + k)),
                  pl.BlockSpec((GLU_HALO, tk), lambda i, k: (jnp.maximum(i * hb - 1, 0), k)),
                  pl.BlockSpec((GLU_HALO, tk), lambda i, k: (jnp.minimum((i + 1) * hb, nhb - 1), k)),
                  pl.BlockSpec((3, tk), lambda i, k: (0, k)),
                  pl.BlockSpec((1, tk), lambda i, k: (0, k)),
                  pl.BlockSpec((tk, d), lambda i, k: (k, 0))],
        out_specs=pl.BlockSpec((tm, d), lambda i, k: (i, 0)),
        compiler_params=_params("arbitrary", "arbitrary"),
        name="glu_down",
    )(u, u, u, u, conv_w, conv_b, w_down)


def _conv_glu(h, w_up, conv_w, conv_b, w_down):
    return _glu_down(_mm(h, w_up, BF16), conv_w, conv_b, w_down)


def _ab_mixer(hl, hc, w_in, rpb, pool_w, pool_scale, w_out, need_ctx):
    n = hl.shape[0]
    naw = w_out.shape[0] // 2
    q_scale = jnp.where(jnp.arange(w_in.shape[1]) < naw, NA_HEAD_DIM ** -0.5 * LOG2E, 1.0)
    w_in = (w_in * q_scale).astype(BF16)
    w_out = w_out.astype(BF16)
    pool_w = pool_w.astype(BF16)
    pool_scale = pool_scale.reshape(1, -1)
    qkv_l = _mm(hl, w_in, BF16, 0, 3 * naw)
    qkv_c = _mm(hc, w_in, BF16, 0, 3 * naw)
    o_a = _na_attention(qkv_l, qkv_c, _na_bias(rpb, n // GRID_W))
    o_b = _group_mm(_pool_diff(_mm(hl, w_in, F32, 3 * naw)), pool_w, pool_scale)
    yl = _mm2(o_a, o_b, w_out, F32)
    yc = None
    if need_ctx:
        o_ac = _dense_attention(qkv_c)
        o_bc = _group_mm(_pool_diff(_mm(hc, w_in, F32, 3 * naw)), pool_w, pool_scale)
        yc = _mm2(o_ac, o_bc, w_out, F32)
    return yl, yc


def _mla_mixer(hl, hc, w_down, g_q, g_kv, w_uq, w_ukv, w_out, tabs_l, tabs_c, need_ctx):
    qr = g_q.shape[0]
    kvr = g_kv.shape[0]
    heads = w_ukv.shape[1] // (MLA_NOPE + MLA_V)
    w_down_ext = jnp.concatenate([w_down[:, :qr + kvr], _rope_cols(w_down[:, qr + kvr:])], axis=1).astype(BF16)
    w_uq_h = w_uq.reshape(qr, heads, MLA_NOPE + MLA_ROPE)
    w_uq_rope = w_uq_h[..., MLA_NOPE:]
    w_uq_ext = jnp.concatenate(
        [w_uq_h[..., :MLA_NOPE], w_uq_rope, w_uq_rope[..., _rope_partner()]],
        axis=-1).reshape(qr, heads * MLA_Q_IN).astype(BF16)
    cs_l = jnp.concatenate([tabs_l[0][:, :MLA_ROPE], tabs_l[1][:, :MLA_ROPE]], axis=1)
    cs_c = jnp.concatenate([tabs_c[0][:, :MLA_ROPE], tabs_c[1][:, :MLA_ROPE]], axis=1)
    w_ukv = w_ukv.astype(BF16)
    w_out = w_out.astype(BF16)
    g_q = g_q.reshape(1, -1)
    g_kv = g_kv.reshape(1, -1)
    qn_l, ckv_l, kr_l = _mla_down(hl, w_down_ext, g_q, g_kv, *tabs_l)
    qn_c, ckv_c, kr_c = _mla_down(hc, w_down_ext, g_q, g_kv, *tabs_c)
    q_l = _q_up(qn_l, w_uq_ext, cs_l, heads)
    kv_l = _mm(ckv_l, w_ukv, BF16)
    kv_c = _mm(ckv_c, w_ukv, BF16)
    yl = _mm(_mla_attention(q_l, kv_l, kr_l, kv_c, kr_c), w_out, F32)
    yc = None
    if need_ctx:
        q_c = _q_up(qn_c, w_uq_ext, cs_c, heads)
        yc = _mm(_mla_attention(q_c, kv_c, kr_c), w_out, F32)
    return yl, yc


def kernel(x, c, ctx, c_ctx, w_ada, b_ada, g_mix_pre, g_mix_post, g_ffn_pre, g_ffn_post, ab_w_in, na_rpb, pool_w, pool_scale, ab_w_out, mla_w_down, mla_g_q, mla_g_kv, mla_w_uq, mla_w_ukv, mla_w_out, ffn_w_up, ffn_conv_w, ffn_conv_b, ffn_w_down):
    batch, n, d = x.shape
    lc = ctx.shape[1]
    depth = w_ada.shape[0]
    assert batch == 1 and c.shape[0] == 1
    xl = x.reshape(n, d)
    xc = ctx.reshape(lc, d)

    cvec = jnp.concatenate([c, c_ctx[None, :], jnp.zeros((6, d), F32)], axis=0)
    mod = _ada(cvec, w_ada, b_ada)

    def mods(l, r):
        return [mod[l, r:r + 1, j * d:(j + 1) * d] for j in range(6)]

    tabs_l = _rope_tables(n)
    tabs_c = (jnp.concatenate([jnp.ones((lc, MLA_ROPE), F32), jnp.zeros((lc, LANES - MLA_ROPE), F32)], axis=1),
              jnp.zeros((lc, LANES), F32))

    vec = lambda a: a.reshape(1, -1)
    sh1, sc1, _, _, _, _ = mods(0, 0)
    sh1c, sc1c, _, _, _, _ = mods(0, 1)
    hl = _norm_mod(xl, vec(g_mix_pre[0]), sc1, sh1)
    hc = _norm_mod(xc, vec(g_mix_pre[0]), sc1c, sh1c)
    for l in range(depth):
        last = l == depth - 1
        sh1, sc1, gt1, sh2, sc2, gt2 = mods(l, 0)
        sh1c, sc1c, gt1c, sh2c, sc2c, gt2c = mods(l, 1)
        if l % 2 == 0:
            e = l // 2
            yl, yc = _ab_mixer(hl, hc, ab_w_in[e], na_rpb[e], pool_w[e], pool_scale[e], ab_w_out[e], not last)
        else:
            o = l // 2
            yl, yc = _mla_mixer(hl, hc, mla_w_down[o], mla_g_q[o], mla_g_kv[o], mla_w_uq[o], mla_w_ukv[o],
                                mla_w_out[o], tabs_l, tabs_c, not last)
        w_up = ffn_w_up[l].astype(BF16)
        w_dn = ffn_w_down[l].astype(BF16)
        conv_b = vec(ffn_conv_b[l])
        xl, h2 = _resid(xl, yl, vec(g_mix_post[l]), gt1, (vec(g_ffn_pre[l]), sc2, sh2))
        f = _conv_glu(h2, w_up, ffn_conv_w[l], conv_b, w_dn)
        if last:
            xl, _ = _resid(xl, f, vec(g_ffn_post[l]), gt2)
        else:
            nsh1, nsc1, _, _, _, _ = mods(l + 1, 0)
            xl, hl = _resid(xl, f, vec(g_ffn_post[l]), gt2, (vec(g_mix_pre[l + 1]), nsc1, nsh1))
            xc, h2c = _resid(xc, yc, vec(g_mix_post[l]), gt1c, (vec(g_ffn_pre[l]), sc2c, sh2c))
            fc = _conv_glu(h2c, w_up, ffn_conv_w[l], conv_b, w_dn)
            nsh1c, nsc1c, _, _, _, _ = mods(l + 1, 1)
            xc, hc = _resid(xc, fc, vec(g_ffn_post[l]), gt2c, (vec(g_mix_pre[l + 1]), nsc1c, nsh1c))
    return xl.reshape(batch, n, d)
```

```python
import functools

import numpy as np
import jax
import jax.numpy as jnp
from jax import lax
from jax.experimental import pallas as pl
from jax.experimental.pallas import tpu as pltpu

GRID_W = 64
EPS = 1e-6
NA_HEAD_DIM = 128
NA_WIN_R = 8
NA_WIN_C = 16
NA_Q_ROWS = 4
NA_K_ROWS = NA_Q_ROWS + NA_WIN_R
POOL_WINDOWS = (2, 4, 8, 16)
POOL_HALO = 8
MLA_NOPE = 128
MLA_ROPE = 64
MLA_V = 128
ROPE_THETA = 10000.0
LANES = 128
NEG = -1e30
LOG2E = 1.4426950408889634
VMEM_LIMIT = 56 * 1024 * 1024

F32 = jnp.float32
BF16 = jnp.bfloat16


def _pick(n, prefs):
    for p in prefs:
        if n % p == 0:
            return p
    return n


def _params(*sem):
    return pltpu.CompilerParams(dimension_semantics=sem, vmem_limit_bytes=VMEM_LIMIT)


def _rms(x):
    return x * lax.rsqrt(jnp.mean(x * x, axis=-1, keepdims=True) + EPS)


def _dot(a, b):
    return jnp.dot(a, b, preferred_element_type=F32)


def _dot_nt(a, b):
    return lax.dot_general(a, b, (((1,), (1,)), ((), ())), preferred_element_type=F32)


def _ada_kernel(c_ref, w_ref, b_ref, o_ref):
    c = c_ref[...]
    s = (c * jax.nn.sigmoid(c)).astype(BF16)
    o_ref[...] = _dot(s, w_ref[...].astype(BF16)) + b_ref[...]


def _ada(cvec, w_ada, b_ada):
    depth, d, n6 = w_ada.shape
    rows = cvec.shape[0]
    tn = _pick(n6, (512, 256, 128))
    return pl.pallas_call(
        _ada_kernel,
        out_shape=jax.ShapeDtypeStruct((depth, rows, n6), F32),
        grid=(depth, n6 // tn),
        in_specs=[
            pl.BlockSpec((rows, d), lambda l, j: (0, 0)),
            pl.BlockSpec((None, d, tn), lambda l, j: (l, 0, j)),
            pl.BlockSpec((None, 1, tn), lambda l, j: (l, 0, j)),
        ],
        out_specs=pl.BlockSpec((None, rows, tn), lambda l, j: (l, 0, j)),
        compiler_params=_params("arbitrary", "arbitrary"),
        name="ada",
    )(cvec, w_ada, b_ada.reshape(depth, 1, n6))


def _norm_mod_kernel(x_ref, g_ref, sc_ref, sh_ref, o_ref):
    y = _rms(x_ref[...]) * g_ref[...]
    o_ref[...] = (y * (1.0 + sc_ref[...]) + sh_ref[...]).astype(o_ref.dtype)


def _norm_mod(x, g, sc, sh):
    m, d = x.shape
    tm = _pick(m, (256, 128, 64, 32, 16))
    vec = pl.BlockSpec((1, d), lambda i: (0, 0))
    return pl.pallas_call(
        _norm_mod_kernel,
        out_shape=jax.ShapeDtypeStruct((m, d), BF16),
        grid=(m // tm,),
        in_specs=[pl.BlockSpec((tm, d), lambda i: (i, 0)), vec, vec, vec],
        out_specs=pl.BlockSpec((tm, d), lambda i: (i, 0)),
        compiler_params=_params("arbitrary"),
        name="norm_mod",
    )(x, g, sc, sh)


def _resid_kernel(x_ref, y_ref, gpost_ref, gt_ref, *rest, with_h):
    xn = x_ref[...] + gt_ref[...] * (_rms(y_ref[...]) * gpost_ref[...])
    if with_h:
        gpre_ref, sc_ref, sh_ref, xo_ref, ho_ref = rest
        xo_ref[...] = xn
        h = _rms(xn) * gpre_ref[...]
        ho_ref[...] = (h * (1.0 + sc_ref[...]) + sh_ref[...]).astype(ho_ref.dtype)
    else:
        (xo_ref,) = rest
        xo_ref[...] = xn


def _resid(x, y, gpost, gt, pre=None):
    m, d = x.shape
    tm = _pick(m, (256, 128, 64, 32, 16))
    vec = pl.BlockSpec((1, d), lambda i: (0, 0))
    row = pl.BlockSpec((tm, d), lambda i: (i, 0))
    with_h = pre is not None
    out_shape = [jax.ShapeDtypeStruct((m, d), F32)]
    out_specs = [row]
    args = [x, y, gpost, gt]
    in_specs = [row, row, vec, vec]
    if with_h:
        out_shape.append(jax.ShapeDtypeStruct((m, d), BF16))
        out_specs.append(row)
        args += list(pre)
        in_specs += [vec, vec, vec]
    out = pl.pallas_call(
        functools.partial(_resid_kernel, with_h=with_h),
        out_shape=out_shape,
        grid=(m // tm,),
        in_specs=in_specs,
        out_specs=out_specs,
        compiler_params=_params("arbitrary"),
        name="resid",
    )(*args)
    return (out[0], out[1]) if with_h else (out[0], None)


def _mm_kernel(a_ref, w_ref, o_ref):
    o_ref[...] = _dot(a_ref[...], w_ref[...]).astype(o_ref.dtype)


def _mm(a, w, out_dtype, col0=0, ncols=None):
    m, k = a.shape
    n = w.shape[1] - col0 if ncols is None else ncols
    tm = _pick(m, (1024, 1280, 512, 256, 128))
    tn = _pick(n, (1024, 512, 256, 128))
    assert col0 % tn == 0
    jb = col0 // tn
    return pl.pallas_call(
        _mm_kernel,
        out_shape=jax.ShapeDtypeStruct((m, n), out_dtype),
        grid=(m // tm, n // tn),
        in_specs=[pl.BlockSpec((tm, k), lambda i, j: (i, 0)),
                  pl.BlockSpec((k, tn), lambda i, j: (0, jb + j))],
        out_specs=pl.BlockSpec((tm, tn), lambda i, j: (i, j)),
        compiler_params=_params("arbitrary", "arbitrary"),
        name="mm",
    )(a, w)


def _mm2_kernel(a1_ref, w1_ref, a2_ref, w2_ref, o_ref):
    o_ref[...] = (_dot(a1_ref[...], w1_ref[...]) + _dot(a2_ref[...], w2_ref[...])).astype(o_ref.dtype)


def _mm2(a1, a2, w, out_dtype):
    m, k1 = a1.shape
    k2 = a2.shape[1]
    n = w.shape[1]
    assert k1 == k2 and w.shape[0] == k1 + k2
    tm = _pick(m, (1024, 512, 256, 128))
    tn = _pick(n, (1024, 512, 256, 128))
    return pl.pallas_call(
        _mm2_kernel,
        out_shape=jax.ShapeDtypeStruct((m, n), out_dtype),
        grid=(m // tm, n // tn),
        in_specs=[pl.BlockSpec((tm, k1), lambda i, j: (i, 0)),
                  pl.BlockSpec((k1, tn), lambda i, j: (0, j)),
                  pl.BlockSpec((tm, k2), lambda i, j: (i, 0)),
                  pl.BlockSpec((k2, tn), lambda i, j: (1, j))],
        out_specs=pl.BlockSpec((tm, tn), lambda i, j: (i, j)),
        compiler_params=_params("arbitrary", "arbitrary"),
        name="mm2",
    )(a1, w, a2, w)


def _na_bias(rpb, rows):
    w = GRID_W
    h = rpb.shape[0]
    qc = np.arange(w)[:, None]
    kc = np.arange(w)[None, :]
    c0 = np.clip(qc - NA_WIN_C // 2, 0, w - NA_WIN_C)
    ok_c = (kc >= c0) & (kc < c0 + NA_WIN_C)
    dc = np.clip(kc - qc + NA_WIN_C - 1, 0, 2 * NA_WIN_C - 2)
    onehot = (dc[None] == np.arange(2 * NA_WIN_C - 1)[:, None, None]).astype(np.float32)
    t = jnp.einsum('hdj,jqk->hdqk', rpb.astype(F32), onehot, precision=lax.Precision.HIGHEST)
    t = jnp.where(ok_c, t, NEG)
    masked = jnp.full((h, w, w), NEG, F32)
    out = []
    for r_first, k_first in ((0, 0), (NA_Q_ROWS, 0), (rows - NA_Q_ROWS, rows - NA_K_ROWS)):
        q_rows = []
        for qr in range(NA_Q_ROWS):
            r = r_first + qr
            r0 = min(max(r - NA_WIN_R // 2, 0), rows - NA_WIN_R)
            blocks = []
            for kr in range(NA_K_ROWS):
                ka = k_first + kr
                blocks.append(t[:, ka - r + NA_WIN_R - 1] if r0 <= ka < r0 + NA_WIN_R else masked)
            q_rows.append(jnp.concatenate(blocks, axis=2))
        out.append(jnp.concatenate(q_rows, axis=1))
    return jnp.stack(out) * LOG2E


def _na_kernel(q_ref, k0_ref, k1_ref, k2_ref, v0_ref, v1_ref, v2_ref, kc_ref, vc_ref, b_ref, o_ref,
               *, heads):
    qb = q_ref.shape[0]
    hd = [slice(h * NA_HEAD_DIM, (h + 1) * NA_HEAD_DIM) for h in range(heads)]
    qs = [q_ref[:, c] for c in hd]
    s_loc = [jnp.concatenate([_dot_nt(q, k_ref[:, c]) for k_ref in (k0_ref, k1_ref, k2_ref)], axis=1)
             + b_ref[h] for h, (q, c) in enumerate(zip(qs, hd))]
    s_ctx = [_dot_nt(q, kc_ref[:, c]) for q, c in zip(qs, hd)]
    ms = [jnp.maximum(jnp.max(sl, axis=1, keepdims=True), jnp.max(sc, axis=1, keepdims=True))
          for sl, sc in zip(s_loc, s_ctx)]
    p_loc = [jnp.exp2(sl - m) for sl, m in zip(s_loc, ms)]
    p_ctx = [jnp.exp2(sc - m) for sc, m in zip(s_ctx, ms)]
    for h, c in enumerate(hd):
        denom = jnp.sum(p_loc[h], axis=1, keepdims=True) + jnp.sum(p_ctx[h], axis=1, keepdims=True)
        o = _dot(p_ctx[h].astype(BF16), vc_ref[:, c])
        for j, v_ref in enumerate((v0_ref, v1_ref, v2_ref)):
            o = o + _dot(p_loc[h][:, j * qb:(j + 1) * qb].astype(BF16), v_ref[:, c])
        o_ref[:, c] = (o / denom).astype(o_ref.dtype)


def _na_attention(qkv_l, qkv_c, bias):
    n = qkv_l.shape[0]
    lc = qkv_c.shape[0]
    width = qkv_l.shape[1] // 3
    heads = width // NA_HEAD_DIM
    qb = NA_Q_ROWS * GRID_W
    nblk = n // qb
    assert n % qb == 0 and nblk >= 3
    hb = _pick(heads, (2, 1))
    hg = heads // hb
    bw = hb * NA_HEAD_DIM

    def kv_spec(j, col0):
        return pl.BlockSpec((qb, bw), lambda h, i: (jnp.clip(i - 1, 0, nblk - 3) + j, col0 + h))

    def bias_map(h, i):
        return (jnp.where(i == 0, 0, jnp.where(i == nblk - 1, 2, 1)), h, 0, 0)

    return pl.pallas_call(
        functools.partial(_na_kernel, heads=hb),
        out_shape=jax.ShapeDtypeStruct((n, width), BF16),
        grid=(hg, nblk),
        in_specs=[
            pl.BlockSpec((qb, bw), lambda h, i: (i, h)),
            kv_spec(0, hg), kv_spec(1, hg), kv_spec(2, hg),
            kv_spec(0, 2 * hg), kv_spec(1, 2 * hg), kv_spec(2, 2 * hg),
            pl.BlockSpec((lc, bw), lambda h, i: (0, hg + h)),
            pl.BlockSpec((lc, bw), lambda h, i: (0, 2 * hg + h)),
            pl.BlockSpec((None, hb, qb, 3 * qb), bias_map),
        ],
        out_specs=pl.BlockSpec((qb, bw), lambda h, i: (i, h)),
        compiler_params=_params("arbitrary", "arbitrary"),
        name="na_attention",
    )(qkv_l, qkv_l, qkv_l, qkv_l, qkv_l, qkv_l, qkv_l, qkv_c, qkv_c, bias)


def _dense_attn_kernel(q_ref, k_ref, v_ref, o_ref):
    s = _dot_nt(q_ref[...], k_ref[...])
    p = jnp.exp2(s - jnp.max(s, axis=1, keepdims=True))
    o = _dot(p.astype(BF16), v_ref[...])
    o_ref[...] = (o / jnp.sum(p, axis=1, keepdims=True)).astype(o_ref.dtype)


def _dense_attention(qkv):
    n = qkv.shape[0]
    width = qkv.shape[1] // 3
    heads = width // NA_HEAD_DIM
    return pl.pallas_call(
        _dense_attn_kernel,
        out_shape=jax.ShapeDtypeStruct((n, width), BF16),
        grid=(heads,),
        in_specs=[pl.BlockSpec((n, NA_HEAD_DIM), lambda h: (0, h)),
                  pl.BlockSpec((n, NA_HEAD_DIM), lambda h: (0, heads + h)),
                  pl.BlockSpec((n, NA_HEAD_DIM), lambda h: (0, 2 * heads + h))],
        out_specs=pl.BlockSpec((n, NA_HEAD_DIM), lambda h: (0, h)),
        compiler_params=_params("arbitrary"),
        name="dense_attention",
    )(qkv, qkv, qkv)


def _pool_kernel(x_ref, o_ref, u_ref, *, n, chunk, blocks_per_group):
    group = pl.program_id(0) // blocks_per_group
    rows = chunk + 2 * POOL_HALO
    u_ref[0:POOL_HALO, :] = jnp.zeros((POOL_HALO, LANES), F32)
    u_ref[n + POOL_HALO:n + 2 * POOL_HALO, :] = jnp.zeros((POOL_HALO, LANES), F32)

    def fill(c, carry):
        base = pl.multiple_of(c * chunk, chunk)
        u_ref[pl.ds(base + POOL_HALO, chunk), :] = x_ref[pl.ds(base, chunk), :]
        return carry
    lax.fori_loop(0, n // chunk, fill, 0)

    def window_sum(xs, w):
        a = pltpu.roll(xs, 1, axis=0) + xs
        half = 1
        while 2 * half < w:
            a = pltpu.roll(a, half, axis=0) + pltpu.roll(a, rows - half, axis=0)
            half *= 2
        return a

    for gi, w in enumerate(POOL_WINDOWS):
        @pl.when(group == gi)
        def _(w=w):
            def body(c, carry):
                base = pl.multiple_of(c * chunk, chunk)
                xs = u_ref[pl.ds(base, rows), :]
                tot = window_sum(xs, w)[POOL_HALO:POOL_HALO + chunk]
                t = base + lax.broadcasted_iota(jnp.int32, (chunk, 1), 0)
                cnt = jnp.minimum(t + w // 2, n) - jnp.maximum(t - w // 2, 0)
                mean = tot / cnt.astype(F32)
                o_ref[pl.ds(base, chunk), :] = (mean - xs[POOL_HALO:POOL_HALO + chunk]).astype(o_ref.dtype)
                return carry
            lax.fori_loop(0, n // chunk, body, 0)


def _pool_diff(u):
    n, c = u.shape
    cg = c // len(POOL_WINDOWS)
    assert cg % LANES == 0
    chunk = _pick(n, (512, 256, 128))
    return pl.pallas_call(
        functools.partial(_pool_kernel, n=n, chunk=chunk, blocks_per_group=cg // LANES),
        out_shape=jax.ShapeDtypeStruct((n, c), BF16),
        grid=(c // LANES,),
        in_specs=[pl.BlockSpec((n, LANES), lambda j: (0, j))],
        out_specs=pl.BlockSpec((n, LANES), lambda j: (0, j)),
        scratch_shapes=[pltpu.VMEM((n + 2 * POOL_HALO, LANES), F32)],
        compiler_params=_params("arbitrary"),
        name="pool_diff",
    )(u)


def _group_mm_kernel(a_ref, w_ref, s_ref, o_ref):
    o_ref[...] = (_dot(a_ref[...], w_ref[...]) * s_ref[...]).astype(o_ref.dtype)


def _group_mm(d, w_grp, ch_scale):
    n, c = d.shape
    g, cg, _ = w_grp.shape
    tm = _pick(n, (1024, 512, 256, 128))
    return pl.pallas_call(
        _group_mm_kernel,
        out_shape=jax.ShapeDtypeStruct((n, c), BF16),
        grid=(g, n // tm),
        in_specs=[pl.BlockSpec((tm, cg), lambda gi, i: (i, gi)),
                  pl.BlockSpec((None, cg, cg), lambda gi, i: (gi, 0, 0)),
                  pl.BlockSpec((1, cg), lambda gi, i: (0, gi))],
        out_specs=pl.BlockSpec((tm, cg), lambda gi, i: (i, gi)),
        compiler_params=_params("arbitrary", "arbitrary"),
        name="group_mm",
    )(d, w_grp, ch_scale)


def _rope_tables(n):
    nf = MLA_ROPE // 4
    t = jnp.arange(n)
    row = (t // GRID_W).astype(F32)
    col = (t % GRID_W).astype(F32)
    freqs = ROPE_THETA ** (-jnp.arange(nf, dtype=F32) / nf)
    ang = jnp.stack([row[:, None] * freqs, col[:, None] * freqs], axis=1)
    cos, sin = jnp.cos(ang), jnp.sin(ang)
    c = jnp.stack([cos, cos], axis=2).reshape(n, MLA_ROPE)
    s = jnp.stack([-sin, sin], axis=2).reshape(n, MLA_ROPE)
    pad = jnp.zeros((n, LANES - MLA_ROPE), F32)
    return jnp.concatenate([c, pad], axis=1), jnp.concatenate([s, pad], axis=1)


def _rope_partner():
    nf = MLA_ROPE // 4
    j = np.arange(MLA_ROPE)
    return (j // (2 * nf)) * (2 * nf) + (1 - (j // nf) % 2) * nf + j % nf


def _rope_cols(w):
    z = jnp.zeros((w.shape[0], LANES - MLA_ROPE), w.dtype)
    return jnp.concatenate([w, z, w[:, _rope_partner()], z], axis=1)


def _mla_down_kernel(h_ref, w_ref, gq_ref, gkv_ref, c_ref, s_ref, qn_ref, ckv_ref, kr_ref, *, qr, kvr):
    z = _dot(h_ref[...], w_ref[...])
    qn_ref[...] = (_rms(z[:, :qr]) * gq_ref[...]).astype(qn_ref.dtype)
    ckv_ref[...] = (_rms(z[:, qr:qr + kvr]) * gkv_ref[...]).astype(ckv_ref.dtype)
    r0 = qr + kvr
    rot = z[:, r0:r0 + LANES] * c_ref[...] + z[:, r0 + LANES:r0 + 2 * LANES] * s_ref[...]
    kr_ref[...] = rot.astype(kr_ref.dtype)


def _mla_down(h, w_ext, g_q, g_kv, ctab, stab):
    n, d = h.shape
    qr = g_q.shape[1]
    kvr = g_kv.shape[1]
    nw = w_ext.shape[1]
    tm = _pick(n, (512, 256, 128))
    return pl.pallas_call(
        functools.partial(_mla_down_kernel, qr=qr, kvr=kvr),
        out_shape=[jax.ShapeDtypeStruct((n, qr), BF16),
                   jax.ShapeDtypeStruct((n, kvr), BF16),
                   jax.ShapeDtypeStruct((n, LANES), BF16)],
        grid=(n // tm,),
        in_specs=[pl.BlockSpec((tm, d), lambda i: (i, 0)),
                  pl.BlockSpec((d, nw), lambda i: (0, 0)),
                  pl.BlockSpec((1, qr), lambda i: (0, 0)),
                  pl.BlockSpec((1, kvr), lambda i: (0, 0)),
                  pl.BlockSpec((tm, LANES), lambda i: (i, 0)),
                  pl.BlockSpec((tm, LANES), lambda i: (i, 0))],
        out_specs=[pl.BlockSpec((tm, qr), lambda i: (i, 0)),
                   pl.BlockSpec((tm, kvr), lambda i: (i, 0)),
                   pl.BlockSpec((tm, LANES), lambda i: (i, 0))],
        compiler_params=_params("arbitrary"),
        name="mla_down",
    )(h, w_ext, g_q, g_kv, ctab, stab)


MLA_Q_IN = MLA_NOPE + LANES
MLA_Q_OUT = MLA_NOPE + LANES
MLA_VX = MLA_V + 16
MLA_JUMP_LIMIT = 64.0


def _q_up_kernel(a_ref, w_ref, cs_ref, o_ref, *, heads, scale):
    z = _dot(a_ref[...], w_ref[...])
    cs = cs_ref[...] * scale
    for j in range(heads):
        zi = j * MLA_Q_IN
        oi = j * MLA_Q_OUT
        o_ref[:, oi:oi + MLA_NOPE] = (z[:, zi:zi + MLA_NOPE] * scale).astype(o_ref.dtype)
        t = z[:, zi + MLA_NOPE:zi + MLA_Q_IN] * cs
        rot = t + pltpu.roll(t, LANES // 2, axis=1)
        o_ref[:, oi + MLA_NOPE:oi + MLA_Q_OUT] = rot.astype(o_ref.dtype)


def _q_up(qn, w_ext, cstab, heads):
    n, qr = qn.shape
    tm = _pick(n, (1024, 512, 256, 128))
    hb = _pick(heads, (4, 2, 1))
    scale = (MLA_NOPE + MLA_ROPE) ** -0.5 * LOG2E
    return pl.pallas_call(
        functools.partial(_q_up_kernel, heads=hb, scale=scale),
        out_shape=jax.ShapeDtypeStruct((n, heads * MLA_Q_OUT), BF16),
        grid=(n // tm, heads // hb),
        in_specs=[pl.BlockSpec((tm, qr), lambda i, j: (i, 0)),
                  pl.BlockSpec((qr, hb * MLA_Q_IN), lambda i, j: (0, j)),
                  pl.BlockSpec((tm, LANES), lambda i, j: (i, 0))],
        out_specs=pl.BlockSpec((tm, hb * MLA_Q_OUT), lambda i, j: (i, j)),
        compiler_params=_params("arbitrary", "arbitrary"),
        name="mla_q_up",
    )(qn, w_ext, cstab)


def _mla_attn_kernel(q_ref, kv_ref, kr_ref, *rest, tk, sub, has_ctx):
    if has_ctx:
        kvc_ref, krc_ref, o_ref, qt_sc, vt_sc, vtc_sc = rest[:6]
    else:
        o_ref, qt_sc, vt_sc = rest[:3]
    slot0, slot1 = rest[-14:-10], rest[-10:-6]
    m_sc, l_sc, acc_sc, accx_sc, ref_sc, jump_sc = rest[-6:]
    tq = q_ref.shape[0]
    nchunks = kv_ref.shape[0] // tk
    cols = [slice(j * sub, (j + 1) * sub) for j in range(tq // sub)]

    @pl.when(pl.program_id(1) == 0)
    def _():
        def tr(c, carry):
            ks = pl.multiple_of(c * tk, tk)
            vt_sc[c, 0:MLA_V, :] = kv_ref[pl.ds(ks, tk), MLA_NOPE:MLA_NOPE + MLA_V].T
            vt_sc[c, MLA_V:MLA_VX, :] = jnp.ones((MLA_VX - MLA_V, tk), BF16)
            return carry
        lax.fori_loop(0, nchunks, tr, 0)
        if has_ctx:
            vtc_sc[0:MLA_V, :] = kvc_ref[:, MLA_NOPE:MLA_NOPE + MLA_V].T
            vtc_sc[MLA_V:MLA_VX, :] = jnp.ones((MLA_VX - MLA_V, vtc_sc.shape[1]), BF16)

    qt_sc[...] = q_ref[...].T

    def keys(c):
        ks = pl.multiple_of(c * tk, tk)
        return jnp.concatenate([kv_ref[pl.ds(ks, tk), 0:MLA_NOPE], kr_ref[pl.ds(ks, tk), :]], axis=1)

    def ctx_keys():
        return jnp.concatenate([kvc_ref[:, 0:MLA_NOPE], krc_ref[...]], axis=1)

    def scores(k):
        return [_dot(k, qt_sc[:, c]) for c in cols]

    def fast_pass():
        sts = scores(ctx_keys())
        m0 = jnp.concatenate([jnp.max(st, axis=0, keepdims=True) for st in sts], axis=1)
        vtc = vtc_sc[...]
        accx_sc[...] = jnp.concatenate(
            [_dot(vtc, jnp.exp2(st - m0[:, c]).astype(BF16)) for st, c in zip(sts, cols)], axis=1)
        m_sc[...] = m0
        ref_sc[...] = m0
        jump_sc[...] = jnp.zeros(jump_sc.shape, F32)

        def qke_stage(c, slot):
            p_sc, r_sc = slot[2], slot[3]
            m_cur = m_sc[...]
            k = keys(c)
            mxs = []
            for col in cols:
                st = _dot(k, qt_sc[:, col])
                p_sc[:, col] = jnp.exp2(st - m_cur[:, col]).astype(BF16)
                mxs.append(jnp.max(st, axis=0, keepdims=True))
            mx = jnp.concatenate(mxs, axis=1)
            r_sc[...] = m_cur
            jump_sc[...] = jnp.maximum(jump_sc[...], mx - m_cur)
            m_sc[...] = jnp.maximum(m_cur, mx)

        def pvx_stage(c, slot):
            p_sc, r_sc = slot[2], slot[3]
            vt = vt_sc[c]
            pv = jnp.concatenate([_dot(vt, p_sc[:, col]) for col in cols], axis=1)
            r = r_sc[...]
            accx_sc[...] = accx_sc[...] * jnp.exp2(ref_sc[...] - r) + pv
            ref_sc[...] = r

        qke_stage(0, slot0)

        def body(i, carry):
            qke_stage(2 * i + 1, slot1)
            pvx_stage(2 * i, slot0)
            qke_stage(2 * i + 2, slot0)
            pvx_stage(2 * i + 1, slot1)
            return carry

        lax.fori_loop(0, nchunks // 2 - 1, body, 0)
        qke_stage(nchunks - 1, slot1)
        pvx_stage(nchunks - 2, slot0)
        pvx_stage(nchunks - 1, slot1)
        accx = accx_sc[...]
        o_ref[...] = (accx[0:MLA_V] / accx[MLA_V:MLA_V + 1]).T.astype(o_ref.dtype)

    if has_ctx and nchunks > 1:
        assert nchunks % 2 == 0
        fast_pass()
        needs_exact = jnp.max(jump_sc[...]) > MLA_JUMP_LIMIT
    else:
        needs_exact = None

    @pl.when(True if needs_exact is None else needs_exact)
    def _():
        _mla_exact_pass(o_ref, kv_ref, kr_ref, kvc_ref if has_ctx else None, krc_ref if has_ctx else None,
                        qt_sc, vt_sc, vtc_sc if has_ctx else None, slot0, slot1, m_sc, l_sc, acc_sc,
                        tk=tk, cols=cols)


def _mla_exact_pass(o_ref, kv_ref, kr_ref, kvc_ref, krc_ref, qt_sc, vt_sc, vtc_sc, slot0, slot1,
                    m_sc, l_sc, acc_sc, *, tk, cols):
    has_ctx = kvc_ref is not None
    nchunks = kv_ref.shape[0] // tk
    assert nchunks == 1 or nchunks % 2 == 0
    m_sc[...] = jnp.full(m_sc.shape, NEG, F32)
    l_sc[...] = jnp.zeros(l_sc.shape, F32)
    acc_sc[...] = jnp.zeros(acc_sc.shape, F32)

    def keys(c):
        ks = pl.multiple_of(c * tk, tk)
        return jnp.concatenate([kv_ref[pl.ds(ks, tk), 0:MLA_NOPE], kr_ref[pl.ds(ks, tk), :]], axis=1)

    def scores(k):
        return [_dot(k, qt_sc[:, c]) for c in cols]

    def softmax_pv(sts, vt):
        m_prev = m_sc[...]
        m_new = jnp.maximum(m_prev, jnp.concatenate(
            [jnp.max(st, axis=0, keepdims=True) for st in sts], axis=1))
        alpha = jnp.exp2(m_prev - m_new)
        sums, pvs = [], []
        for st, c in zip(sts, cols):
            p = jnp.exp2(st - m_new[:, c])
            sums.append(jnp.sum(p, axis=0, keepdims=True))
            pvs.append(_dot(vt, p.astype(BF16)))
        l_sc[...] = alpha * l_sc[...] + jnp.concatenate(sums, axis=1)
        acc_sc[...] = alpha * acc_sc[...] + jnp.concatenate(pvs, axis=1)
        m_sc[...] = m_new

    def qk_stage(c, slot):
        st_sc, mx_sc = slot[0], slot[1]
        for col, st in zip(cols, scores(keys(c))):
            st_sc[:, col] = st
            mx_sc[:, col] = jnp.max(st, axis=0, keepdims=True)

    def softmax_stage(slot):
        st_sc, mx_sc, p_sc, al_sc = slot
        m_prev = m_sc[...]
        m_new = jnp.maximum(m_prev, mx_sc[...])
        alpha = jnp.exp2(m_prev - m_new)
        sums = []
        for col in cols:
            p = jnp.exp2(st_sc[:, col] - m_new[:, col])
            sums.append(jnp.sum(p, axis=0, keepdims=True))
            p_sc[:, col] = p.astype(BF16)
        l_sc[...] = alpha * l_sc[...] + jnp.concatenate(sums, axis=1)
        al_sc[...] = alpha
        m_sc[...] = m_new

    def pv_stage(c, slot):
        p_sc, al_sc = slot[2], slot[3]
        vt = vt_sc[c, 0:MLA_V, :]
        pv = jnp.concatenate([_dot(vt, p_sc[:, col]) for col in cols], axis=1)
        acc_sc[...] = al_sc[...] * acc_sc[...] + pv

    if has_ctx:
        kc = jnp.concatenate([kvc_ref[:, 0:MLA_NOPE], krc_ref[...]], axis=1)
        softmax_pv(scores(kc), vtc_sc[0:MLA_V, :])
    if nchunks == 1:
        softmax_pv(scores(keys(0)), vt_sc[0, 0:MLA_V, :])
    else:
        qk_stage(0, slot0)
        qk_stage(1, slot1)
        softmax_stage(slot0)

        def body(i, carry):
            softmax_stage(slot1)
            qk_stage(2 * i + 2, slot0)
            pv_stage(2 * i, slot0)
            softmax_stage(slot0)
            qk_stage(2 * i + 3, slot1)
            pv_stage(2 * i + 1, slot1)
            return carry

        lax.fori_loop(0, nchunks // 2 - 1, body, 0)
        softmax_stage(slot1)
        pv_stage(nchunks - 2, slot0)
        pv_stage(nchunks - 1, slot1)
    o_ref[...] = (acc_sc[...] / l_sc[...]).T.astype(o_ref.dtype)


def _mla_attention(q, kv, kr, kv_c=None, kr_c=None):
    nq = q.shape[0]
    s = kv.shape[0]
    heads = q.shape[1] // MLA_Q_OUT
    tq = _pick(nq, (2048, 1024, 512, 256, 128))
    sub = _pick(tq, (512, 256, 128))
    tk = _pick(s, (512, 256, 128))
    has_ctx = kv_c is not None
    kvw = MLA_NOPE + MLA_V
    in_specs = [pl.BlockSpec((tq, MLA_Q_OUT), lambda h, i: (i, h)),
                pl.BlockSpec((s, kvw), lambda h, i: (0, h)),
                pl.BlockSpec((s, LANES), lambda h, i: (0, 0))]
    args = [q, kv, kr]
    scratch = [pltpu.VMEM((MLA_Q_OUT, tq), BF16), pltpu.VMEM((s // tk, MLA_VX, tk), BF16)]
    if has_ctx:
        sc = kv_c.shape[0]
        in_specs += [pl.BlockSpec((sc, kvw), lambda h, i: (0, h)),
                     pl.BlockSpec((sc, LANES), lambda h, i: (0, 0))]
        args += [kv_c, kr_c]
        scratch.append(pltpu.VMEM((MLA_VX, sc), BF16))
    slot = [pltpu.VMEM((tk, tq), F32), pltpu.VMEM((1, tq), F32), pltpu.VMEM((tk, tq), BF16), pltpu.VMEM((1, tq), F32)]
    scratch += slot + slot
    scratch += [pltpu.VMEM((1, tq), F32), pltpu.VMEM((1, tq), F32), pltpu.VMEM((MLA_V, tq), F32),
                pltpu.VMEM((MLA_VX, tq), F32), pltpu.VMEM((1, tq), F32), pltpu.VMEM((1, tq), F32)]
    return pl.pallas_call(
        functools.partial(_mla_attn_kernel, tk=tk, sub=sub, has_ctx=has_ctx),
        out_shape=jax.ShapeDtypeStruct((nq, heads * MLA_V), BF16),
        grid=(heads, nq // tq),
        in_specs=in_specs,
        out_specs=pl.BlockSpec((tq, MLA_V), lambda h, i: (i, h)),
        scratch_shapes=scratch,
        compiler_params=_params("arbitrary", "arbitrary"),
        name="mla_attention",
    )(*args)


GLU_HALO = 16


def _glu_down_kernel(g_ref, v_ref, gp_ref, gn_ref, cw_ref, cb_ref, w_ref, o_ref):
    i = pl.program_id(0)
    k = pl.program_id(1)
    tm = g_ref.shape[0]

    @pl.when(k == 0)
    def _():
        o_ref[...] = jnp.zeros(o_ref.shape, F32)

    g = g_ref[...].astype(F32)
    prev_row = jnp.where(i > 0, gp_ref[GLU_HALO - 1:GLU_HALO, :].astype(F32), 0.0)
    next_row = jnp.where(i < pl.num_programs(0) - 1, gn_ref[0:1, :].astype(F32), 0.0)
    row = lax.broadcasted_iota(jnp.int32, (tm, 1), 0)
    g_dn = jnp.where(row == 0, prev_row, pltpu.roll(g, 1, axis=0))
    g_up = jnp.where(row == tm - 1, next_row, pltpu.roll(g, tm - 1, axis=0))
    cw = cw_ref[...]
    gc = g_dn * cw[0:1, :] + g * cw[1:2, :] + g_up * cw[2:3, :] + cb_ref[...]
    a = (gc * jax.nn.sigmoid(gc)) * v_ref[...].astype(F32)
    o_ref[...] = _dot(a.astype(BF16), w_ref[...]) + o_ref[...]


def _glu_down(u, conv_w, conv_b, w_down):
    n = u.shape[0]
    dff, d = w_down.shape
    tm = _pick(n, (1024, 512, 256, 128))
    tk = _pick(dff, (512, 256, 128))
    nk = dff // tk
    hb = tm // GLU_HALO
    nhb = n // GLU_HALO
    return pl.pallas_call(
        _glu_down_kernel,
        out_shape=jax.ShapeDtypeStruct((n, d), F32),
        grid=(n // tm, nk),
        in_specs=[pl.BlockSpec((tm, tk), lambda i, k: (i, k)),
                  pl.BlockSpec((tm, tk), lambda i, k: (i, nk + k)),
                  pl.BlockSpec((GLU_HALO, tk), lambda i, k: (jnp.maximum(i * hb - 1, 0), k)),
                  pl.BlockSpec((GLU_HALO, tk), lambda i, k: (jnp.minimum((i + 1) * hb, nhb - 1), k)),
                  pl.BlockSpec((3, tk), lambda i, k: (0, k)),
                  pl.BlockSpec((1, tk), lambda i, k: (0, k)),
                  pl.BlockSpec((tk, d), lambda i, k: (k, 0))],
        out_specs=pl.BlockSpec((tm, d), lambda i, k: (i, 0)),
        compiler_params=_params("arbitrary", "arbitrary"),
        name="glu_down",
    )(u, u, u, u, conv_w, conv_b, w_down)


def _conv_glu(h, w_up, conv_w, conv_b, w_down):
    return _glu_down(_mm(h, w_up, BF16), conv_w, conv_b, w_down)


def _ab_mixer(hl, hc, w_in, rpb, pool_w, pool_scale, w_out, need_ctx):
    n = hl.shape[0]
    naw = w_out.shape[0] // 2
    q_scale = jnp.where(jnp.arange(w_in.shape[1]) < naw, NA_HEAD_DIM ** -0.5 * LOG2E, 1.0)
    w_in = (w_in * q_scale).astype(BF16)
    w_out = w_out.astype(BF16)
    pool_w = pool_w.astype(BF16)
    pool_scale = pool_scale.reshape(1, -1)
    qkv_l = _mm(hl, w_in, BF16, 0, 3 * naw)
    qkv_c = _mm(hc, w_in, BF16, 0, 3 * naw)
    o_a = _na_attention(qkv_l, qkv_c, _na_bias(rpb, n // GRID_W))
    o_b = _group_mm(_pool_diff(_mm(hl, w_in, F32, 3 * naw)), pool_w, pool_scale)
    yl = _mm2(o_a, o_b, w_out, F32)
    yc = None
    if need_ctx:
        o_ac = _dense_attention(qkv_c)
        o_bc = _group_mm(_pool_diff(_mm(hc, w_in, F32, 3 * naw)), pool_w, pool_scale)
        yc = _mm2(o_ac, o_bc, w_out, F32)
    return yl, yc


def _mla_mixer(hl, hc, w_down, g_q, g_kv, w_uq, w_ukv, w_out, tabs_l, tabs_c, need_ctx):
    qr = g_q.shape[0]
    kvr = g_kv.shape[0]
    heads = w_ukv.shape[1] // (MLA_NOPE + MLA_V)
    w_down_ext = jnp.concatenate([w_down[:, :qr + kvr], _rope_cols(w_down[:, qr + kvr:])], axis=1).astype(BF16)
    w_uq_h = w_uq.reshape(qr, heads, MLA_NOPE + MLA_ROPE)
    w_uq_rope = w_uq_h[..., MLA_NOPE:]
    w_uq_ext = jnp.concatenate(
        [w_uq_h[..., :MLA_NOPE], w_uq_rope, w_uq_rope[..., _rope_partner()]],
        axis=-1).reshape(qr, heads * MLA_Q_IN).astype(BF16)
    cs_l = jnp.concatenate([tabs_l[0][:, :MLA_ROPE], tabs_l[1][:, :MLA_ROPE]], axis=1)
    cs_c = jnp.concatenate([tabs_c[0][:, :MLA_ROPE], tabs_c[1][:, :MLA_ROPE]], axis=1)
    w_ukv = w_ukv.astype(BF16)
    w_out = w_out.astype(BF16)
    g_q = g_q.reshape(1, -1)
    g_kv = g_kv.reshape(1, -1)
    qn_l, ckv_l, kr_l = _mla_down(hl, w_down_ext, g_q, g_kv, *tabs_l)
    qn_c, ckv_c, kr_c = _mla_down(hc, w_down_ext, g_q, g_kv, *tabs_c)
    q_l = _q_up(qn_l, w_uq_ext, cs_l, heads)
    kv_l = _mm(ckv_l, w_ukv, BF16)
    kv_c = _mm(ckv_c, w_ukv, BF16)
    yl = _mm(_mla_attention(q_l, kv_l, kr_l, kv_c, kr_c), w_out, F32)
    yc = None
    if need_ctx:
        q_c = _q_up(qn_c, w_uq_ext, cs_c, heads)
        yc = _mm(_mla_attention(q_c, kv_c, kr_c), w_out, F32)
    return yl, yc


def kernel(x, c, ctx, c_ctx, w_ada, b_ada, g_mix_pre, g_mix_post, g_ffn_pre, g_ffn_post, ab_w_in, na_rpb, pool_w, pool_scale, ab_w_out, mla_w_down, mla_g_q, mla_g_kv, mla_w_uq, mla_w_ukv, mla_w_out, ffn_w_up, ffn_conv_w, ffn_conv_b, ffn_w_down):
    batch, n, d = x.shape
    lc = ctx.shape[1]
    depth = w_ada.shape[0]
    assert batch == 1 and c.shape[0] == 1
    xl = x.reshape(n, d)
    xc = ctx.reshape(lc, d)

    cvec = jnp.concatenate([c, c_ctx[None, :], jnp.zeros((6, d), F32)], axis=0)
    mod = _ada(cvec, w_ada, b_ada)

    def mods(l, r):
        return [mod[l, r:r + 1, j * d:(j + 1) * d] for j in range(6)]

    tabs_l = _rope_tables(n)
    tabs_c = (jnp.concatenate([jnp.ones((lc, MLA_ROPE), F32), jnp.zeros((lc, LANES - MLA_ROPE), F32)], axis=1),
              jnp.zeros((lc, LANES), F32))

    vec = lambda a: a.reshape(1, -1)
    sh1, sc1, _, _, _, _ = mods(0, 0)
    sh1c, sc1c, _, _, _, _ = mods(0, 1)
    hl = _norm_mod(xl, vec(g_mix_pre[0]), sc1, sh1)
    hc = _norm_mod(xc, vec(g_mix_pre[0]), sc1c, sh1c)
    for l in range(depth):
        last = l == depth - 1
        sh1, sc1, gt1, sh2, sc2, gt2 = mods(l, 0)
        sh1c, sc1c, gt1c, sh2c, sc2c, gt2c = mods(l, 1)
        if l % 2 == 0:
            e = l // 2
            yl, yc = _ab_mixer(hl, hc, ab_w_in[e], na_rpb[e], pool_w[e], pool_scale[e], ab_w_out[e], not last)
        else:
            o = l // 2
            yl, yc = _mla_mixer(hl, hc, mla_w_down[o], mla_g_q[o], mla_g_kv[o], mla_w_uq[o], mla_w_ukv[o],
                                mla_w_out[o], tabs_l, tabs_c, not last)
        w_up = ffn_w_up[l].astype(BF16)
        w_dn = ffn_w_down[l].astype(BF16)
        conv_b = vec(ffn_conv_b[l])
        xl, h2 = _resid(xl, yl, vec(g_mix_post[l]), gt1, (vec(g_ffn_pre[l]), sc2, sh2))
        f = _conv_glu(h2, w_up, ffn_conv_w[l], conv_b, w_dn)
        if last:
            xl, _ = _resid(xl, f, vec(g_ffn_post[l]), gt2)
        else:
            nsh1, nsc1, _, _, _, _ = mods(l + 1, 0)
            xl, hl = _resid(xl, f, vec(g_ffn_post[l]), gt2, (vec(g_mix_pre[l + 1]), nsc1, nsh1))
            xc, h2c = _resid(xc, yc, vec(g_mix_post[l]), gt1c, (vec(g_ffn_pre[l]), sc2c, sh2c))
            fc = _conv_glu(h2c, w_up, ffn_conv_w[l], conv_b, w_dn)
            nsh1c, nsc1c, _, _, _, _ = mods(l + 1, 1)
            xc, hc = _resid(xc, fc, vec(g_ffn_post[l]), gt2c, (vec(g_mix_pre[l + 1]), nsc1c, nsh1c))
    return xl.reshape(batch, n, d)
```

```python
import functools

import numpy as np
import jax
import jax.numpy as jnp
from jax import lax
from jax.experimental import pallas as pl
from jax.experimental.pallas import tpu as pltpu

GRID_W = 64
EPS = 1e-6
NA_HEAD_DIM = 128
NA_WIN_R = 8
NA_WIN_C = 16
NA_Q_ROWS = 4
NA_K_ROWS = NA_Q_ROWS + NA_WIN_R
POOL_WINDOWS = (2, 4, 8, 16)
POOL_HALO = 8
MLA_NOPE = 128
MLA_ROPE = 64
MLA_V = 128
ROPE_THETA = 10000.0
LANES = 128
NEG = -1e30
LOG2E = 1.4426950408889634
VMEM_LIMIT = 56 * 1024 * 1024

F32 = jnp.float32
BF16 = jnp.bfloat16


def _pick(n, prefs):
    for p in prefs:
        if n % p == 0:
            return p
    return n


def _params(*sem):
    return pltpu.CompilerParams(dimension_semantics=sem, vmem_limit_bytes=VMEM_LIMIT)


def _rms(x):
    return x * lax.rsqrt(jnp.mean(x * x, axis=-1, keepdims=True) + EPS)


def _dot(a, b):
    return jnp.dot(a, b, preferred_element_type=F32)


def _dot_nt(a, b):
    return lax.dot_general(a, b, (((1,), (1,)), ((), ())), preferred_element_type=F32)


def _ada_kernel(c_ref, w_ref, b_ref, o_ref):
    c = c_ref[...]
    s = (c * jax.nn.sigmoid(c)).astype(BF16)
    o_ref[...] = _dot(s, w_ref[...].astype(BF16)) + b_ref[...]


def _ada(cvec, w_ada, b_ada):
    depth, d, n6 = w_ada.shape
    rows = cvec.shape[0]
    tn = _pick(n6, (512, 256, 128))
    return pl.pallas_call(
        _ada_kernel,
        out_shape=jax.ShapeDtypeStruct((depth, rows, n6), F32),
        grid=(depth, n6 // tn),
        in_specs=[
            pl.BlockSpec((rows, d), lambda l, j: (0, 0)),
            pl.BlockSpec((None, d, tn), lambda l, j: (l, 0, j)),
            pl.BlockSpec((None, 1, tn), lambda l, j: (l, 0, j)),
        ],
        out_specs=pl.BlockSpec((None, rows, tn), lambda l, j: (l, 0, j)),
        compiler_params=_params("arbitrary", "arbitrary"),
        name="ada",
    )(cvec, w_ada, b_ada.reshape(depth, 1, n6))


def _norm_mod_kernel(x_ref, g_ref, sc_ref, sh_ref, o_ref):
    y = _rms(x_ref[...]) * g_ref[...]
    o_ref[...] = (y * (1.0 + sc_ref[...]) + sh_ref[...]).astype(o_ref.dtype)


def _norm_mod(x, g, sc, sh):
    m, d = x.shape
    tm = _pick(m, (256, 128, 64, 32, 16))
    vec = pl.BlockSpec((1, d), lambda i: (0, 0))
    return pl.pallas_call(
        _norm_mod_kernel,
        out_shape=jax.ShapeDtypeStruct((m, d), BF16),
        grid=(m // tm,),
        in_specs=[pl.BlockSpec((tm, d), lambda i: (i, 0)), vec, vec, vec],
        out_specs=pl.BlockSpec((tm, d), lambda i: (i, 0)),
        compiler_params=_params("arbitrary"),
        name="norm_mod",
    )(x, g, sc, sh)


def _resid_kernel(x_ref, y_ref, gpost_ref, gt_ref, *rest, with_h):
    xn = x_ref[...] + gt_ref[...] * (_rms(y_ref[...].astype(F32)) * gpost_ref[...])
    if with_h:
        gpre_ref, sc_ref, sh_ref, xo_ref, ho_ref = rest
        xo_ref[...] = xn
        h = _rms(xn) * gpre_ref[...]
        ho_ref[...] = (h * (1.0 + sc_ref[...]) + sh_ref[...]).astype(ho_ref.dtype)
    else:
        (xo_ref,) = rest
        xo_ref[...] = xn


def _resid(x, y, gpost, gt, pre=None):
    m, d = x.shape
    tm = _pick(m, (256, 128, 64, 32, 16))
    vec = pl.BlockSpec((1, d), lambda i: (0, 0))
    row = pl.BlockSpec((tm, d), lambda i: (i, 0))
    with_h = pre is not None
    out_shape = [jax.ShapeDtypeStruct((m, d), F32)]
    out_specs = [row]
    args = [x, y, gpost, gt]
    in_specs = [row, row, vec, vec]
    if with_h:
        out_shape.append(jax.ShapeDtypeStruct((m, d), BF16))
        out_specs.append(row)
        args += list(pre)
        in_specs += [vec, vec, vec]
    out = pl.pallas_call(
        functools.partial(_resid_kernel, with_h=with_h),
        out_shape=out_shape,
        grid=(m // tm,),
        in_specs=in_specs,
        out_specs=out_specs,
        compiler_params=_params("arbitrary"),
        name="resid",
    )(*args)
    return (out[0], out[1]) if with_h else (out[0], None)


def _mm_kernel(a_ref, w_ref, o_ref):
    o_ref[...] = _dot(a_ref[...], w_ref[...]).astype(o_ref.dtype)


def _mm(a, w, out_dtype, col0=0, ncols=None):
    m, k = a.shape
    n = w.shape[1] - col0 if ncols is None else ncols
    tm = _pick(m, (1024, 1280, 512, 256, 128))
    tn = _pick(n, (1024, 512, 256, 128))
    assert col0 % tn == 0
    jb = col0 // tn
    return pl.pallas_call(
        _mm_kernel,
        out_shape=jax.ShapeDtypeStruct((m, n), out_dtype),
        grid=(m // tm, n // tn),
        in_specs=[pl.BlockSpec((tm, k), lambda i, j: (i, 0)),
                  pl.BlockSpec((k, tn), lambda i, j: (0, jb + j))],
        out_specs=pl.BlockSpec((tm, tn), lambda i, j: (i, j)),
        compiler_params=_params("arbitrary", "arbitrary"),
        name="mm",
    )(a, w)


def _mm2_kernel(a1_ref, w1_ref, a2_ref, w2_ref, o_ref):
    o_ref[...] = (_dot(a1_ref[...], w1_ref[...]) + _dot(a2_ref[...], w2_ref[...])).astype(o_ref.dtype)


def _mm2(a1, a2, w, out_dtype):
    m, k1 = a1.shape
    k2 = a2.shape[1]
    n = w.shape[1]
    assert k1 == k2 and w.shape[0] == k1 + k2
    tm = _pick(m, (1024, 512, 256, 128))
    tn = _pick(n, (1024, 512, 256, 128))
    return pl.pallas_call(
        _mm2_kernel,
        out_shape=jax.ShapeDtypeStruct((m, n), out_dtype),
        grid=(m // tm, n // tn),
        in_specs=[pl.BlockSpec((tm, k1), lambda i, j: (i, 0)),
                  pl.BlockSpec((k1, tn), lambda i, j: (0, j)),
                  pl.BlockSpec((tm, k2), lambda i, j: (i, 0)),
                  pl.BlockSpec((k2, tn), lambda i, j: (1, j))],
        out_specs=pl.BlockSpec((tm, tn), lambda i, j: (i, j)),
        compiler_params=_params("arbitrary", "arbitrary"),
        name="mm2",
    )(a1, w, a2, w)


def _na_bias(rpb, rows):
    w = GRID_W
    h = rpb.shape[0]
    qc = np.arange(w)[:, None]
    kc = np.arange(w)[None, :]
    c0 = np.clip(qc - NA_WIN_C // 2, 0, w - NA_WIN_C)
    ok_c = (kc >= c0) & (kc < c0 + NA_WIN_C)
    dc = np.clip(kc - qc + NA_WIN_C - 1, 0, 2 * NA_WIN_C - 2)
    onehot = (dc[None] == np.arange(2 * NA_WIN_C - 1)[:, None, None]).astype(np.float32)
    t = jnp.einsum('hdj,jqk->hdqk', rpb.astype(F32), onehot, precision=lax.Precision.HIGHEST)
    t = jnp.where(ok_c, t, NEG)
    masked = jnp.full((h, w, w), NEG, F32)
    out = []
    for r_first, k_first in ((0, 0), (NA_Q_ROWS, 0), (rows - NA_Q_ROWS, rows - NA_K_ROWS)):
        q_rows = []
        for qr in range(NA_Q_ROWS):
            r = r_first + qr
            r0 = min(max(r - NA_WIN_R // 2, 0), rows - NA_WIN_R)
            blocks = []
            for kr in range(NA_K_ROWS):
                ka = k_first + kr
                blocks.append(t[:, ka - r + NA_WIN_R - 1] if r0 <= ka < r0 + NA_WIN_R else masked)
            q_rows.append(jnp.concatenate(blocks, axis=2))
        out.append(jnp.concatenate(q_rows, axis=1))
    return jnp.stack(out) * LOG2E


def _na_kernel(q_ref, k0_ref, k1_ref, k2_ref, v0_ref, v1_ref, v2_ref, kc_ref, vc_ref, b_ref, o_ref,
               *, heads):
    qb = q_ref.shape[0]
    hd = [slice(h * NA_HEAD_DIM, (h + 1) * NA_HEAD_DIM) for h in range(heads)]
    qs = [q_ref[:, c] for c in hd]
    s_loc = [jnp.concatenate([_dot_nt(q, k_ref[:, c]) for k_ref in (k0_ref, k1_ref, k2_ref)], axis=1)
             + b_ref[h] for h, (q, c) in enumerate(zip(qs, hd))]
    s_ctx = [_dot_nt(q, kc_ref[:, c]) for q, c in zip(qs, hd)]
    ms = [jnp.maximum(jnp.max(sl, axis=1, keepdims=True), jnp.max(sc, axis=1, keepdims=True))
          for sl, sc in zip(s_loc, s_ctx)]
    p_loc = [jnp.exp2(sl - m) for sl, m in zip(s_loc, ms)]
    p_ctx = [jnp.exp2(sc - m) for sc, m in zip(s_ctx, ms)]
    for h, c in enumerate(hd):
        denom = jnp.sum(p_loc[h], axis=1, keepdims=True) + jnp.sum(p_ctx[h], axis=1, keepdims=True)
        o = _dot(p_ctx[h].astype(BF16), vc_ref[:, c])
        for j, v_ref in enumerate((v0_ref, v1_ref, v2_ref)):
            o = o + _dot(p_loc[h][:, j * qb:(j + 1) * qb].astype(BF16), v_ref[:, c])
        o_ref[:, c] = (o / denom).astype(o_ref.dtype)


def _na_attention(qkv_l, qkv_c, bias):
    n = qkv_l.shape[0]
    lc = qkv_c.shape[0]
    width = qkv_l.shape[1] // 3
    heads = width // NA_HEAD_DIM
    qb = NA_Q_ROWS * GRID_W
    nblk = n // qb
    assert n % qb == 0 and nblk >= 3
    hb = _pick(heads, (2, 1))
    hg = heads // hb
    bw = hb * NA_HEAD_DIM

    def kv_spec(j, col0):
        return pl.BlockSpec((qb, bw), lambda h, i: (jnp.clip(i - 1, 0, nblk - 3) + j, col0 + h))

    def bias_map(h, i):
        return (jnp.where(i == 0, 0, jnp.where(i == nblk - 1, 2, 1)), h, 0, 0)

    return pl.pallas_call(
        functools.partial(_na_kernel, heads=hb),
        out_shape=jax.ShapeDtypeStruct((n, width), BF16),
        grid=(hg, nblk),
        in_specs=[
            pl.BlockSpec((qb, bw), lambda h, i: (i, h)),
            kv_spec(0, hg), kv_spec(1, hg), kv_spec(2, hg),
            kv_spec(0, 2 * hg), kv_spec(1, 2 * hg), kv_spec(2, 2 * hg),
            pl.BlockSpec((lc, bw), lambda h, i: (0, hg + h)),
            pl.BlockSpec((lc, bw), lambda h, i: (0, 2 * hg + h)),
            pl.BlockSpec((None, hb, qb, 3 * qb), bias_map),
        ],
        out_specs=pl.BlockSpec((qb, bw), lambda h, i: (i, h)),
        compiler_params=_params("arbitrary", "arbitrary"),
        name="na_attention",
    )(qkv_l, qkv_l, qkv_l, qkv_l, qkv_l, qkv_l, qkv_l, qkv_c, qkv_c, bias)


def _dense_attn_kernel(q_ref, k_ref, v_ref, o_ref):
    s = _dot_nt(q_ref[...], k_ref[...])
    p = jnp.exp2(s - jnp.max(s, axis=1, keepdims=True))
    o = _dot(p.astype(BF16), v_ref[...])
    o_ref[...] = (o / jnp.sum(p, axis=1, keepdims=True)).astype(o_ref.dtype)


def _dense_attention(qkv):
    n = qkv.shape[0]
    width = qkv.shape[1] // 3
    heads = width // NA_HEAD_DIM
    return pl.pallas_call(
        _dense_attn_kernel,
        out_shape=jax.ShapeDtypeStruct((n, width), BF16),
        grid=(heads,),
        in_specs=[pl.BlockSpec((n, NA_HEAD_DIM), lambda h: (0, h)),
                  pl.BlockSpec((n, NA_HEAD_DIM), lambda h: (0, heads + h)),
                  pl.BlockSpec((n, NA_HEAD_DIM), lambda h: (0, 2 * heads + h))],
        out_specs=pl.BlockSpec((n, NA_HEAD_DIM), lambda h: (0, h)),
        compiler_params=_params("arbitrary"),
        name="dense_attention",
    )(qkv, qkv, qkv)


def _pool_kernel(x_ref, o_ref, u_ref, *, n, chunk, blocks_per_group):
    group = pl.program_id(0) // blocks_per_group
    rows = chunk + 2 * POOL_HALO
    u_ref[0:POOL_HALO, :] = jnp.zeros((POOL_HALO, LANES), F32)
    u_ref[n + POOL_HALO:n + 2 * POOL_HALO, :] = jnp.zeros((POOL_HALO, LANES), F32)

    def fill(c, carry):
        base = pl.multiple_of(c * chunk, chunk)
        u_ref[pl.ds(base + POOL_HALO, chunk), :] = x_ref[pl.ds(base, chunk), :]
        return carry
    lax.fori_loop(0, n // chunk, fill, 0)

    def window_sum(xs, w):
        a = pltpu.roll(xs, 1, axis=0) + xs
        half = 1
        while 2 * half < w:
            a = pltpu.roll(a, half, axis=0) + pltpu.roll(a, rows - half, axis=0)
            half *= 2
        return a

    for gi, w in enumerate(POOL_WINDOWS):
        @pl.when(group == gi)
        def _(w=w):
            def body(c, carry):
                base = pl.multiple_of(c * chunk, chunk)
                xs = u_ref[pl.ds(base, rows), :]
                tot = window_sum(xs, w)[POOL_HALO:POOL_HALO + chunk]
                t = base + lax.broadcasted_iota(jnp.int32, (chunk, 1), 0)
                cnt = jnp.minimum(t + w // 2, n) - jnp.maximum(t - w // 2, 0)
                mean = tot / cnt.astype(F32)
                o_ref[pl.ds(base, chunk), :] = (mean - xs[POOL_HALO:POOL_HALO + chunk]).astype(o_ref.dtype)
                return carry
            lax.fori_loop(0, n // chunk, body, 0)


def _pool_diff(u):
    n, c = u.shape
    cg = c // len(POOL_WINDOWS)
    assert cg % LANES == 0
    chunk = _pick(n, (512, 256, 128))
    return pl.pallas_call(
        functools.partial(_pool_kernel, n=n, chunk=chunk, blocks_per_group=cg // LANES),
        out_shape=jax.ShapeDtypeStruct((n, c), BF16),
        grid=(c // LANES,),
        in_specs=[pl.BlockSpec((n, LANES), lambda j: (0, j))],
        out_specs=pl.BlockSpec((n, LANES), lambda j: (0, j)),
        scratch_shapes=[pltpu.VMEM((n + 2 * POOL_HALO, LANES), F32)],
        compiler_params=_params("arbitrary"),
        name="pool_diff",
    )(u)


def _group_mm_kernel(a_ref, w_ref, s_ref, o_ref):
    o_ref[...] = (_dot(a_ref[...], w_ref[...]) * s_ref[...]).astype(o_ref.dtype)


def _group_mm(d, w_grp, ch_scale):
    n, c = d.shape
    g, cg, _ = w_grp.shape
    tm = _pick(n, (1024, 512, 256, 128))
    return pl.pallas_call(
        _group_mm_kernel,
        out_shape=jax.ShapeDtypeStruct((n, c), BF16),
        grid=(g, n // tm),
        in_specs=[pl.BlockSpec((tm, cg), lambda gi, i: (i, gi)),
                  pl.BlockSpec((None, cg, cg), lambda gi, i: (gi, 0, 0)),
                  pl.BlockSpec((1, cg), lambda gi, i: (0, gi))],
        out_specs=pl.BlockSpec((tm, cg), lambda gi, i: (i, gi)),
        compiler_params=_params("arbitrary", "arbitrary"),
        name="group_mm",
    )(d, w_grp, ch_scale)


def _rope_tables(n):
    nf = MLA_ROPE // 4
    t = jnp.arange(n)
    row = (t // GRID_W).astype(F32)
    col = (t % GRID_W).astype(F32)
    freqs = ROPE_THETA ** (-jnp.arange(nf, dtype=F32) / nf)
    ang = jnp.stack([row[:, None] * freqs, col[:, None] * freqs], axis=1)
    cos, sin = jnp.cos(ang), jnp.sin(ang)
    c = jnp.stack([cos, cos], axis=2).reshape(n, MLA_ROPE)
    s = jnp.stack([-sin, sin], axis=2).reshape(n, MLA_ROPE)
    pad = jnp.zeros((n, LANES - MLA_ROPE), F32)
    return jnp.concatenate([c, pad], axis=1), jnp.concatenate([s, pad], axis=1)


def _rope_partner():
    nf = MLA_ROPE // 4
    j = np.arange(MLA_ROPE)
    return (j // (2 * nf)) * (2 * nf) + (1 - (j // nf) % 2) * nf + j % nf


def _rope_cols(w):
    z = jnp.zeros((w.shape[0], LANES - MLA_ROPE), w.dtype)
    return jnp.concatenate([w, z, w[:, _rope_partner()], z], axis=1)


def _mla_down_kernel(h_ref, w_ref, gq_ref, gkv_ref, c_ref, s_ref, qn_ref, ckv_ref, kr_ref, *, qr, kvr):
    z = _dot(h_ref[...], w_ref[...])
    qn_ref[...] = (_rms(z[:, :qr]) * gq_ref[...]).astype(qn_ref.dtype)
    ckv_ref[...] = (_rms(z[:, qr:qr + kvr]) * gkv_ref[...]).astype(ckv_ref.dtype)
    r0 = qr + kvr
    rot = z[:, r0:r0 + LANES] * c_ref[...] + z[:, r0 + LANES:r0 + 2 * LANES] * s_ref[...]
    kr_ref[...] = rot.astype(kr_ref.dtype)


def _mla_down(h, w_ext, g_q, g_kv, ctab, stab):
    n, d = h.shape
    qr = g_q.shape[1]
    kvr = g_kv.shape[1]
    nw = w_ext.shape[1]
    tm = _pick(n, (512, 256, 128))
    return pl.pallas_call(
        functools.partial(_mla_down_kernel, qr=qr, kvr=kvr),
        out_shape=[jax.ShapeDtypeStruct((n, qr), BF16),
                   jax.ShapeDtypeStruct((n, kvr), BF16),
                   jax.ShapeDtypeStruct((n, LANES), BF16)],
        grid=(n // tm,),
        in_specs=[pl.BlockSpec((tm, d), lambda i: (i, 0)),
                  pl.BlockSpec((d, nw), lambda i: (0, 0)),
                  pl.BlockSpec((1, qr), lambda i: (0, 0)),
                  pl.BlockSpec((1, kvr), lambda i: (0, 0)),
                  pl.BlockSpec((tm, LANES), lambda i: (i, 0)),
                  pl.BlockSpec((tm, LANES), lambda i: (i, 0))],
        out_specs=[pl.BlockSpec((tm, qr), lambda i: (i, 0)),
                   pl.BlockSpec((tm, kvr), lambda i: (i, 0)),
                   pl.BlockSpec((tm, LANES), lambda i: (i, 0))],
        compiler_params=_params("arbitrary"),
        name="mla_down",
    )(h, w_ext, g_q, g_kv, ctab, stab)


MLA_Q_IN = MLA_NOPE + LANES
MLA_Q_OUT = MLA_NOPE + LANES
MLA_VX = MLA_V + 16
MLA_JUMP_LIMIT = 64.0


def _q_up_kernel(a_ref, w_ref, cs_ref, o_ref, *, heads, scale):
    z = _dot(a_ref[...], w_ref[...])
    cs = cs_ref[...] * scale
    for j in range(heads):
        zi = j * MLA_Q_IN
        oi = j * MLA_Q_OUT
        o_ref[:, oi:oi + MLA_NOPE] = (z[:, zi:zi + MLA_NOPE] * scale).astype(o_ref.dtype)
        t = z[:, zi + MLA_NOPE:zi + MLA_Q_IN] * cs
        rot = t + pltpu.roll(t, LANES // 2, axis=1)
        o_ref[:, oi + MLA_NOPE:oi + MLA_Q_OUT] = rot.astype(o_ref.dtype)


def _q_up(qn, w_ext, cstab, heads):
    n, qr = qn.shape
    tm = _pick(n, (1024, 512, 256, 128))
    hb = _pick(heads, (4, 2, 1))
    scale = (MLA_NOPE + MLA_ROPE) ** -0.5 * LOG2E
    return pl.pallas_call(
        functools.partial(_q_up_kernel, heads=hb, scale=scale),
        out_shape=jax.ShapeDtypeStruct((n, heads * MLA_Q_OUT), BF16),
        grid=(n // tm, heads // hb),
        in_specs=[pl.BlockSpec((tm, qr), lambda i, j: (i, 0)),
                  pl.BlockSpec((qr, hb * MLA_Q_IN), lambda i, j: (0, j)),
                  pl.BlockSpec((tm, LANES), lambda i, j: (i, 0))],
        out_specs=pl.BlockSpec((tm, hb * MLA_Q_OUT), lambda i, j: (i, j)),
        compiler_params=_params("arbitrary", "arbitrary"),
        name="mla_q_up",
    )(qn, w_ext, cstab)


def _mla_attn_kernel(q_ref, kv_ref, kr_ref, *rest, tk, sub, has_ctx):
    if has_ctx:
        kvc_ref, krc_ref, o_ref, qt_sc, vt_sc, vtc_sc = rest[:6]
    else:
        o_ref, qt_sc, vt_sc = rest[:3]
    slot0, slot1 = rest[-14:-10], rest[-10:-6]
    m_sc, l_sc, acc_sc, accx_sc, ref_sc, jump_sc = rest[-6:]
    tq = q_ref.shape[0]
    nchunks = kv_ref.shape[0] // tk
    cols = [slice(j * sub, (j + 1) * sub) for j in range(tq // sub)]

    @pl.when(pl.program_id(1) == 0)
    def _():
        def tr(c, carry):
            ks = pl.multiple_of(c * tk, tk)
            vt_sc[c, 0:MLA_V, :] = kv_ref[pl.ds(ks, tk), MLA_NOPE:MLA_NOPE + MLA_V].T
            vt_sc[c, MLA_V:MLA_VX, :] = jnp.ones((MLA_VX - MLA_V, tk), BF16)
            return carry
        lax.fori_loop(0, nchunks, tr, 0)
        if has_ctx:
            vtc_sc[0:MLA_V, :] = kvc_ref[:, MLA_NOPE:MLA_NOPE + MLA_V].T
            vtc_sc[MLA_V:MLA_VX, :] = jnp.ones((MLA_VX - MLA_V, vtc_sc.shape[1]), BF16)

    qt_sc[...] = q_ref[...].T

    def keys(c):
        ks = pl.multiple_of(c * tk, tk)
        return jnp.concatenate([kv_ref[pl.ds(ks, tk), 0:MLA_NOPE], kr_ref[pl.ds(ks, tk), :]], axis=1)

    def ctx_keys():
        return jnp.concatenate([kvc_ref[:, 0:MLA_NOPE], krc_ref[...]], axis=1)

    def scores(k):
        return [_dot(k, qt_sc[:, c]) for c in cols]

    def fast_pass():
        sts = scores(ctx_keys())
        m0 = jnp.concatenate([jnp.max(st, axis=0, keepdims=True) for st in sts], axis=1)
        vtc = vtc_sc[...]
        accx_sc[...] = jnp.concatenate(
            [_dot(vtc, jnp.exp2(st - m0[:, c]).astype(BF16)) for st, c in zip(sts, cols)], axis=1)
        m_sc[...] = m0
        ref_sc[...] = m0
        jump_sc[...] = jnp.zeros(jump_sc.shape, F32)

        def qke_stage(c, slot):
            p_sc, r_sc = slot[2], slot[3]
            m_cur = m_sc[...]
            k = keys(c)
            mxs = []
            for col in cols:
                st = _dot(k, qt_sc[:, col])
                p_sc[:, col] = jnp.exp2(st - m_cur[:, col]).astype(BF16)
                mxs.append(jnp.max(st, axis=0, keepdims=True))
            mx = jnp.concatenate(mxs, axis=1)
            r_sc[...] = m_cur
            jump_sc[...] = jnp.maximum(jump_sc[...], mx - m_cur)
            m_sc[...] = jnp.maximum(m_cur, mx)

        def pvx_stage(c, slot):
            p_sc, r_sc = slot[2], slot[3]
            vt = vt_sc[c]
            pv = jnp.concatenate([_dot(vt, p_sc[:, col]) for col in cols], axis=1)
            r = r_sc[...]
            accx_sc[...] = accx_sc[...] * jnp.exp2(ref_sc[...] - r) + pv
            ref_sc[...] = r

        qke_stage(0, slot0)

        def body(i, carry):
            qke_stage(2 * i + 1, slot1)
            pvx_stage(2 * i, slot0)
            qke_stage(2 * i + 2, slot0)
            pvx_stage(2 * i + 1, slot1)
            return carry

        lax.fori_loop(0, nchunks // 2 - 1, body, 0)
        qke_stage(nchunks - 1, slot1)
        pvx_stage(nchunks - 2, slot0)
        pvx_stage(nchunks - 1, slot1)
        accx = accx_sc[...]
        o_ref[...] = (accx[0:MLA_V] / accx[MLA_V:MLA_V + 1]).T.astype(o_ref.dtype)

    if has_ctx and nchunks > 1:
        assert nchunks % 2 == 0
        fast_pass()
        needs_exact = jnp.max(jump_sc[...]) > MLA_JUMP_LIMIT
    else:
        needs_exact = None

    @pl.when(True if needs_exact is None else needs_exact)
    def _():
        _mla_exact_pass(o_ref, kv_ref, kr_ref, kvc_ref if has_ctx else None, krc_ref if has_ctx else None,
                        qt_sc, vt_sc, vtc_sc if has_ctx else None, slot0, slot1, m_sc, l_sc, acc_sc,
                        tk=tk, cols=cols)


def _mla_exact_pass(o_ref, kv_ref, kr_ref, kvc_ref, krc_ref, qt_sc, vt_sc, vtc_sc, slot0, slot1,
                    m_sc, l_sc, acc_sc, *, tk, cols):
    has_ctx = kvc_ref is not None
    nchunks = kv_ref.shape[0] // tk
    assert nchunks == 1 or nchunks % 2 == 0
    m_sc[...] = jnp.full(m_sc.shape, NEG, F32)
    l_sc[...] = jnp.zeros(l_sc.shape, F32)
    acc_sc[...] = jnp.zeros(acc_sc.shape, F32)

    def keys(c):
        ks = pl.multiple_of(c * tk, tk)
        return jnp.concatenate([kv_ref[pl.ds(ks, tk), 0:MLA_NOPE], kr_ref[pl.ds(ks, tk), :]], axis=1)

    def scores(k):
        return [_dot(k, qt_sc[:, c]) for c in cols]

    def softmax_pv(sts, vt):
        m_prev = m_sc[...]
        m_new = jnp.maximum(m_prev, jnp.concatenate(
            [jnp.max(st, axis=0, keepdims=True) for st in sts], axis=1))
        alpha = jnp.exp2(m_prev - m_new)
        sums, pvs = [], []
        for st, c in zip(sts, cols):
            p = jnp.exp2(st - m_new[:, c])
            sums.append(jnp.sum(p, axis=0, keepdims=True))
            pvs.append(_dot(vt, p.astype(BF16)))
        l_sc[...] = alpha * l_sc[...] + jnp.concatenate(sums, axis=1)
        acc_sc[...] = alpha * acc_sc[...] + jnp.concatenate(pvs, axis=1)
        m_sc[...] = m_new

    def qk_stage(c, slot):
        st_sc, mx_sc = slot[0], slot[1]
        for col, st in zip(cols, scores(keys(c))):
            st_sc[:, col] = st
            mx_sc[:, col] = jnp.max(st, axis=0, keepdims=True)

    def softmax_stage(slot):
        st_sc, mx_sc, p_sc, al_sc = slot
        m_prev = m_sc[...]
        m_new = jnp.maximum(m_prev, mx_sc[...])
        alpha = jnp.exp2(m_prev - m_new)
        sums = []
        for col in cols:
            p = jnp.exp2(st_sc[:, col] - m_new[:, col])
            sums.append(jnp.sum(p, axis=0, keepdims=True))
            p_sc[:, col] = p.astype(BF16)
        l_sc[...] = alpha * l_sc[...] + jnp.concatenate(sums, axis=1)
        al_sc[...] = alpha
        m_sc[...] = m_new

    def pv_stage(c, slot):
        p_sc, al_sc = slot[2], slot[3]
        vt = vt_sc[c, 0:MLA_V, :]
        pv = jnp.concatenate([_dot(vt, p_sc[:, col]) for col in cols], axis=1)
        acc_sc[...] = al_sc[...] * acc_sc[...] + pv

    if has_ctx:
        kc = jnp.concatenate([kvc_ref[:, 0:MLA_NOPE], krc_ref[...]], axis=1)
        softmax_pv(scores(kc), vtc_sc[0:MLA_V, :])
    if nchunks == 1:
        softmax_pv(scores(keys(0)), vt_sc[0, 0:MLA_V, :])
    else:
        qk_stage(0, slot0)
        qk_stage(1, slot1)
        softmax_stage(slot0)

        def body(i, carry):
            softmax_stage(slot1)
            qk_stage(2 * i + 2, slot0)
            pv_stage(2 * i, slot0)
            softmax_stage(slot0)
            qk_stage(2 * i + 3, slot1)
            pv_stage(2 * i + 1, slot1)
            return carry

        lax.fori_loop(0, nchunks // 2 - 1, body, 0)
        softmax_stage(slot1)
        pv_stage(nchunks - 2, slot0)
        pv_stage(nchunks - 1, slot1)
    o_ref[...] = (acc_sc[...] / l_sc[...]).T.astype(o_ref.dtype)


def _mla_attention(q, kv, kr, kv_c=None, kr_c=None):
    nq = q.shape[0]
    s = kv.shape[0]
    heads = q.shape[1] // MLA_Q_OUT
    tq = _pick(nq, (2048, 1024, 512, 256, 128))
    sub = _pick(tq, (512, 256, 128))
    tk = _pick(s, (512, 256, 128))
    has_ctx = kv_c is not None
    kvw = MLA_NOPE + MLA_V
    in_specs = [pl.BlockSpec((tq, MLA_Q_OUT), lambda h, i: (i, h)),
                pl.BlockSpec((s, kvw), lambda h, i: (0, h)),
                pl.BlockSpec((s, LANES), lambda h, i: (0, 0))]
    args = [q, kv, kr]
    scratch = [pltpu.VMEM((MLA_Q_OUT, tq), BF16), pltpu.VMEM((s // tk, MLA_VX, tk), BF16)]
    if has_ctx:
        sc = kv_c.shape[0]
        in_specs += [pl.BlockSpec((sc, kvw), lambda h, i: (0, h)),
                     pl.BlockSpec((sc, LANES), lambda h, i: (0, 0))]
        args += [kv_c, kr_c]
        scratch.append(pltpu.VMEM((MLA_VX, sc), BF16))
    slot = [pltpu.VMEM((tk, tq), F32), pltpu.VMEM((1, tq), F32), pltpu.VMEM((tk, tq), BF16), pltpu.VMEM((1, tq), F32)]
    scratch += slot + slot
    scratch += [pltpu.VMEM((1, tq), F32), pltpu.VMEM((1, tq), F32), pltpu.VMEM((MLA_V, tq), F32),
                pltpu.VMEM((MLA_VX, tq), F32), pltpu.VMEM((1, tq), F32), pltpu.VMEM((1, tq), F32)]
    return pl.pallas_call(
        functools.partial(_mla_attn_kernel, tk=tk, sub=sub, has_ctx=has_ctx),
        out_shape=jax.ShapeDtypeStruct((nq, heads * MLA_V), BF16),
        grid=(heads, nq // tq),
        in_specs=in_specs,
        out_specs=pl.BlockSpec((tq, MLA_V), lambda h, i: (i, h)),
        scratch_shapes=scratch,
        compiler_params=_params("arbitrary", "arbitrary"),
        name="mla_attention",
    )(*args)


GLU_HALO = 16


def _glu_down_kernel(g_ref, v_ref, gp_ref, gn_ref, cw_ref, cb_ref, w_ref, o_ref):
    i = pl.program_id(0)
    k = pl.program_id(1)
    tm = g_ref.shape[0]

    @pl.when(k == 0)
    def _():
        o_ref[...] = jnp.zeros(o_ref.shape, F32)

    g = g_ref[...].astype(F32)
    prev_row = jnp.where(i > 0, gp_ref[GLU_HALO - 1:GLU_HALO, :].astype(F32), 0.0)
    next_row = jnp.where(i < pl.num_programs(0) - 1, gn_ref[0:1, :].astype(F32), 0.0)
    row = lax.broadcasted_iota(jnp.int32, (tm, 1), 0)
    g_dn = jnp.where(row == 0, prev_row, pltpu.roll(g, 1, axis=0))
    g_up = jnp.where(row == tm - 1, next_row, pltpu.roll(g, tm - 1, axis=0))
    cw = cw_ref[...]
    gc = g_dn * cw[0:1, :] + g * cw[1:2, :] + g_up * cw[2:3, :] + cb_ref[...]
    a = (gc * jax.nn.sigmoid(gc)) * v_ref[...].astype(F32)
    o_ref[...] = _dot(a.astype(BF16), w_ref[...]) + o_ref[...]


def _glu_down(u, conv_w, conv_b, w_down):
    n = u.shape[0]
    dff, d = w_down.shape
    tm = _pick(n, (512, 256, 128))
    tk = _pick(dff, (1024, 512, 256, 128))
    nk = dff // tk
    hb = tm // GLU_HALO
    nhb = n // GLU_HALO
    return pl.pallas_call(
        _glu_down_kernel,
        out_shape=jax.ShapeDtypeStruct((n, d), F32),
        grid=(n // tm, nk),
        in_specs=[pl.BlockSpec((tm, tk), lambda i, k: (i, k)),
                  pl.BlockSpec((tm, tk), lambda i, k: (i, nk + k)),
                  pl.BlockSpec((GLU_HALO, tk), lambda i, k: (jnp.maximum(i * hb - 1, 0), k)),
                  pl.BlockSpec((GLU_HALO, tk), lambda i, k: (jnp.minimum((i + 1) * hb, nhb - 1), k)),
                  pl.BlockSpec((3, tk), lambda i, k: (0, k)),
                  pl.BlockSpec((1, tk), lambda i, k: (0, k)),
                  pl.BlockSpec((tk, d), lambda i, k: (k, 0))],
        out_specs=pl.BlockSpec((tm, d), lambda i, k: (i, 0)),
        compiler_params=_params("arbitrary", "arbitrary"),
        name="glu_down",
    )(u, u, u, u, conv_w, conv_b, w_down)


def _conv_glu(h, w_up, conv_w, conv_b, w_down):
    return _glu_down(_mm(h, w_up, BF16), conv_w, conv_b, w_down)


def _ab_mixer(hl, hc, w_in, rpb, pool_w, pool_scale, w_out, need_ctx):
    n = hl.shape[0]
    naw = w_out.shape[0] // 2
    q_scale = jnp.where(jnp.arange(w_in.shape[1]) < naw, NA_HEAD_DIM ** -0.5 * LOG2E, 1.0)
    w_in = (w_in * q_scale).astype(BF16)
    w_out = w_out.astype(BF16)
    pool_w = pool_w.astype(BF16)
    pool_scale = pool_scale.reshape(1, -1)
    qkv_l = _mm(hl, w_in, BF16, 0, 3 * naw)
    qkv_c = _mm(hc, w_in, BF16, 0, 3 * naw)
    o_a = _na_attention(qkv_l, qkv_c, _na_bias(rpb, n // GRID_W))
    o_b = _group_mm(_pool_diff(_mm(hl, w_in, F32, 3 * naw)), pool_w, pool_scale)
    yl = _mm2(o_a, o_b, w_out, BF16)
    yc = None
    if need_ctx:
        o_ac = _dense_attention(qkv_c)
        o_bc = _group_mm(_pool_diff(_mm(hc, w_in, F32, 3 * naw)), pool_w, pool_scale)
        yc = _mm2(o_ac, o_bc, w_out, BF16)
    return yl, yc


def _mla_mixer(hl, hc, w_down, g_q, g_kv, w_uq, w_ukv, w_out, tabs_l, tabs_c, need_ctx):
    qr = g_q.shape[0]
    kvr = g_kv.shape[0]
    heads = w_ukv.shape[1] // (MLA_NOPE + MLA_V)
    w_down_ext = jnp.concatenate([w_down[:, :qr + kvr], _rope_cols(w_down[:, qr + kvr:])], axis=1).astype(BF16)
    w_uq_h = w_uq.reshape(qr, heads, MLA_NOPE + MLA_ROPE)
    w_uq_rope = w_uq_h[..., MLA_NOPE:]
    w_uq_ext = jnp.concatenate(
        [w_uq_h[..., :MLA_NOPE], w_uq_rope, w_uq_rope[..., _rope_partner()]],
        axis=-1).reshape(qr, heads * MLA_Q_IN).astype(BF16)
    cs_l = jnp.concatenate([tabs_l[0][:, :MLA_ROPE], tabs_l[1][:, :MLA_ROPE]], axis=1)
    cs_c = jnp.concatenate([tabs_c[0][:, :MLA_ROPE], tabs_c[1][:, :MLA_ROPE]], axis=1)
    w_ukv = w_ukv.astype(BF16)
    w_out = w_out.astype(BF16)
    g_q = g_q.reshape(1, -1)
    g_kv = g_kv.reshape(1, -1)
    qn_l, ckv_l, kr_l = _mla_down(hl, w_down_ext, g_q, g_kv, *tabs_l)
    qn_c, ckv_c, kr_c = _mla_down(hc, w_down_ext, g_q, g_kv, *tabs_c)
    q_l = _q_up(qn_l, w_uq_ext, cs_l, heads)
    kv_l = _mm(ckv_l, w_ukv, BF16)
    kv_c = _mm(ckv_c, w_ukv, BF16)
    yl = _mm(_mla_attention(q_l, kv_l, kr_l, kv_c, kr_c), w_out, BF16)
    yc = None
    if need_ctx:
        q_c = _q_up(qn_c, w_uq_ext, cs_c, heads)
        yc = _mm(_mla_attention(q_c, kv_c, kr_c), w_out, BF16)
    return yl, yc


def kernel(x, c, ctx, c_ctx, w_ada, b_ada, g_mix_pre, g_mix_post, g_ffn_pre, g_ffn_post, ab_w_in, na_rpb, pool_w, pool_scale, ab_w_out, mla_w_down, mla_g_q, mla_g_kv, mla_w_uq, mla_w_ukv, mla_w_out, ffn_w_up, ffn_conv_w, ffn_conv_b, ffn_w_down):
    batch, n, d = x.shape
    lc = ctx.shape[1]
    depth = w_ada.shape[0]
    assert batch == 1 and c.shape[0] == 1
    xl = x.reshape(n, d)
    xc = ctx.reshape(lc, d)

    cvec = jnp.concatenate([c, c_ctx[None, :], jnp.zeros((6, d), F32)], axis=0)
    mod = _ada(cvec, w_ada, b_ada)

    def mods(l, r):
        return [mod[l, r:r + 1, j * d:(j + 1) * d] for j in range(6)]

    tabs_l = _rope_tables(n)
    tabs_c = (jnp.concatenate([jnp.ones((lc, MLA_ROPE), F32), jnp.zeros((lc, LANES - MLA_ROPE), F32)], axis=1),
              jnp.zeros((lc, LANES), F32))

    vec = lambda a: a.reshape(1, -1)
    sh1, sc1, _, _, _, _ = mods(0, 0)
    sh1c, sc1c, _, _, _, _ = mods(0, 1)
    hl = _norm_mod(xl, vec(g_mix_pre[0]), sc1, sh1)
    hc = _norm_mod(xc, vec(g_mix_pre[0]), sc1c, sh1c)
    for l in range(depth):
        last = l == depth - 1
        sh1, sc1, gt1, sh2, sc2, gt2 = mods(l, 0)
        sh1c, sc1c, gt1c, sh2c, sc2c, gt2c = mods(l, 1)
        if l % 2 == 0:
            e = l // 2
            yl, yc = _ab_mixer(hl, hc, ab_w_in[e], na_rpb[e], pool_w[e], pool_scale[e], ab_w_out[e], not last)
        else:
            o = l // 2
            yl, yc = _mla_mixer(hl, hc, mla_w_down[o], mla_g_q[o], mla_g_kv[o], mla_w_uq[o], mla_w_ukv[o],
                                mla_w_out[o], tabs_l, tabs_c, not last)
        w_up = ffn_w_up[l].astype(BF16)
        w_dn = ffn_w_down[l].astype(BF16)
        conv_b = vec(ffn_conv_b[l])
        xl, h2 = _resid(xl, yl, vec(g_mix_post[l]), gt1, (vec(g_ffn_pre[l]), sc2, sh2))
        f = _conv_glu(h2, w_up, ffn_conv_w[l], conv_b, w_dn)
        if last:
            xl, _ = _resid(xl, f, vec(g_ffn_post[l]), gt2)
        else:
            nsh1, nsc1, _, _, _, _ = mods(l + 1, 0)
            xl, hl = _resid(xl, f, vec(g_ffn_post[l]), gt2, (vec(g_mix_pre[l + 1]), nsc1, nsh1))
            xc, h2c = _resid(xc, yc, vec(g_mix_post[l]), gt1c, (vec(g_ffn_pre[l]), sc2c, sh2c))
            fc = _conv_glu(h2c, w_up, ffn_conv_w[l], conv_b, w_dn)
            nsh1c, nsc1c, _, _, _, _ = mods(l + 1, 1)
            xc, hc = _resid(xc, fc, vec(g_ffn_post[l]), gt2c, (vec(g_mix_pre[l + 1]), nsc1c, nsh1c))
    return xl.reshape(batch, n, d)
```
